```python
import jax, jax.numpy as jnp
from jax import lax
import numpy as np

D_MODEL = 2048
BATCH = 4
SEQ = 8192
DEPTH = 2
DEC_BATCH = 8
DEC_SEQ = 4096
PAST_LEN = 128

GRID_W = 64
POOL_WINDOWS = (2, 4, 8, 16)
POOL_GROUPS = len(POOL_WINDOWS)
D_POOL = D_MODEL // 2
POOL_GC = D_POOL // POOL_GROUPS
CHUNK = 128
SGU_GROUPS = 8
D_SGU = D_MODEL // 2
SGU_GC = D_SGU // SGU_GROUPS
NA_HEADS = 16
NA_HEAD_DIM = 64
D_NA = NA_HEADS * NA_HEAD_DIM
NA_KH_MAX = 8
NA_KW = 16
N_BRANCH = 3
D_FF = 4 * D_MODEL
D_IN = D_POOL + 2 * D_SGU + 3 * D_NA + N_BRANCH * D_MODEL
SPLITS = tuple(int(i) for i in np.cumsum([D_POOL, D_SGU, D_SGU, D_NA, D_NA, D_NA]))
DN_ALPHA = (2 * DEPTH) ** 0.25
DN_BETA = (8 * DEPTH) ** -0.25
LN_EPS = 1e-5

kernel_name = "hybrid_pool_sgu_natten_encoder"


def layer_norm(x, g, b):
    x32 = x.astype(jnp.float32)
    mu = jnp.mean(x32, axis=-1, keepdims=True)
    var = jnp.mean(jnp.square(x32 - mu), axis=-1, keepdims=True)
    y = (x32 - mu) * lax.rsqrt(var + LN_EPS)
    return (y * g.astype(jnp.float32) + b.astype(jnp.float32)).astype(x.dtype)


def pool_mixer(a, w_pool, s_pool):
    bsz, s, _ = a.shape
    a32 = a.astype(jnp.float32).reshape(bsz, s, POOL_GROUPS, POOL_GC)
    cs = jnp.concatenate([jnp.zeros((bsz, 1, POOL_GROUPS, POOL_GC), jnp.float32),
                          jnp.cumsum(a32, axis=1)], axis=1)
    t = np.arange(s)
    outs = []
    for gi, w in enumerate(POOL_WINDOWS):
        lo = np.clip(t - w // 2, 0, s)
        hi = np.clip(t + w // 2, 0, s)
        inv_cnt = (1.0 / (hi - lo)).astype(np.float32)[None, :, None]
        csg = cs[:, :, gi]
        outs.append((csg[:, hi] - csg[:, lo]) * inv_cnt - a32[:, :, gi])
    p = jnp.stack(outs, axis=2).astype(a.dtype)
    y = jnp.einsum('bsgc,gcd->bsgd', p, w_pool).reshape(bsz, s, D_POOL)
    return y * s_pool


def sgu_mixer(u, v, ln_g, ln_b, w_s, b_s):
    bsz, s, _ = u.shape
    u = jax.nn.gelu(u)
    v = layer_norm(jax.nn.gelu(v), ln_g, ln_b)
    vc = v.reshape(bsz, s // CHUNK, CHUNK, SGU_GROUPS, SGU_GC)
    sg = jnp.einsum('gpq,bnqgc->bnpgc', w_s, vc) + b_s.T[None, None, :, :, None]
    return u * sg.reshape(bsz, s, D_SGU)


def neighbourhood_attention(q, k, v, rpb):
    bsz, s, h, dh = q.shape
    rows = s // GRID_W
    kh = min(NA_KH_MAX, rows)
    qg = q.reshape(bsz, rows, GRID_W, h, dh)
    kg = k.reshape(bsz, rows, GRID_W, h, dh)
    vg = v.reshape(bsz, rows, GRID_W, h, dh)
    col = np.arange(GRID_W)
    c0 = np.clip(col - NA_KW // 2, 0, GRID_W - NA_KW)
    col_idx = c0[:, None] + np.arange(NA_KW)[None, :]
    dc = col_idx - col[:, None]
    scale = dh ** -0.5

    def one_row(r):
        r0 = jnp.clip(r - kh // 2, 0, rows - kh)
        q_r = lax.dynamic_index_in_dim(qg, r, axis=1, keepdims=False)
        k_band = lax.dynamic_slice_in_dim(kg, r0, kh, axis=1)
        v_band = lax.dynamic_slice_in_dim(vg, r0, kh, axis=1)
        k_sel = k_band[:, :, col_idx]
        v_sel = v_band[:, :, col_idx]
        dr = r0 + jnp.arange(kh) - r
        bias = rpb[:, dr[:, None, None] + (NA_KH_MAX - 1), dc[None] + (NA_KW - 1)]
        sc = jnp.einsum('bqhd,biqjhd->bhqij', q_r, k_sel).astype(jnp.float32) * scale
        sc = sc + bias.astype(jnp.float32).transpose(0, 2, 1, 3)[None]
        p = jax.nn.softmax(sc.reshape(bsz, h, GRID_W, kh * NA_KW), axis=-1)
        p = p.reshape(bsz, h, GRID_W, kh, NA_KW).astype(v.dtype)
        return jnp.einsum('bhqij,biqjhd->bqhd', p, v_sel)

    out = lax.map(one_row, jnp.arange(rows))
    return out.transpose(1, 0, 2, 3, 4).reshape(bsz, s, h * dh)


def mixing_sublayer(x, w_in, w_pool, s_pool, sgu_ln_g, sgu_ln_b, w_s, b_s, rpb,
                    w_br_pool, w_br_sgu, w_br_na, w_out):
    bsz, s, _ = x.shape
    z = x @ w_in
    a, u, v, q, k, vv, g = jnp.split(z, SPLITS, axis=-1)
    y_a = pool_mixer(a, w_pool, s_pool) @ w_br_pool
    y_b = sgu_mixer(u, v, sgu_ln_g, sgu_ln_b, w_s, b_s) @ w_br_sgu
    hd = (bsz, s, NA_HEADS, NA_HEAD_DIM)
    y_c = neighbourhood_attention(q.reshape(hd), k.reshape(hd), vv.reshape(hd), rpb) @ w_br_na
    gates = jax.nn.sigmoid(g.reshape(bsz, s, N_BRANCH, D_MODEL))
    merged = gates[:, :, 0] * y_a + gates[:, :, 1] * y_b + gates[:, :, 2] * y_c
    return merged @ w_out


def trunk(x, w_in, w_pool, s_pool, sgu_ln_g, sgu_ln_b, w_s, b_s, rpb,
          w_br_pool, w_br_sgu, w_br_na, w_out, ln1_g, ln1_b, w_up, w_down, ln2_g, ln2_b):
    for l in range(DEPTH):
        m = mixing_sublayer(x, w_in[l], w_pool[l], s_pool[l], sgu_ln_g[l], sgu_ln_b[l], w_s[l], b_s[l],
                            rpb[l], w_br_pool[l], w_br_sgu[l], w_br_na[l], w_out[l])
        x = layer_norm(DN_ALPHA * x + m, ln1_g[l], ln1_b[l])
        f = jnp.square(jax.nn.relu(x @ w_up[l])) @ w_down[l]
        x = layer_norm(DN_ALPHA * x + f, ln2_g[l], ln2_b[l])
    return x


def setup_inputs(seed: int = 0) -> dict:
    key = jax.random.key(seed)
    ks = jax.random.split(key, 24)
    f32 = jnp.float32
    nrm = lambda k, shape, sc: jax.random.normal(k, shape, f32) * sc
    L = DEPTH
    return {
        "x_prompt": nrm(ks[0], (BATCH, SEQ, D_MODEL), 1.0),
        "x_sample": nrm(ks[1], (DEC_BATCH, DEC_SEQ, D_MODEL), 1.0),
        "w_in": nrm(ks[2], (L, D_MODEL, D_IN), D_MODEL ** -0.5),
        "w_pool": nrm(ks[3], (L, POOL_GROUPS, POOL_GC, POOL_GC), POOL_GC ** -0.5),
        "s_pool": 1.0 + nrm(ks[4], (L, D_POOL), 0.1),
        "sgu_ln_g": 1.0 + nrm(ks[5], (L, D_SGU), 0.01),
        "sgu_ln_b": nrm(ks[6], (L, D_SGU), 0.01),
        "w_s": nrm(ks[7], (L, SGU_GROUPS, CHUNK, CHUNK), CHUNK ** -0.5),
        "b_s": 1.0 + nrm(ks[8], (L, SGU_GROUPS, CHUNK), 0.01),
        "rpb": nrm(ks[9], (L, NA_HEADS, 2 * NA_KH_MAX - 1, 2 * NA_KW - 1), 0.1),
        "w_br_pool": nrm(ks[10], (L, D_POOL, D_MODEL), DN_BETA * D_POOL ** -0.5),
        "w_br_sgu": nrm(ks[11], (L, D_SGU, D_MODEL), DN_BETA * D_SGU ** -0.5),
        "w_br_na": nrm(ks[12], (L, D_NA, D_MODEL), DN_BETA * D_NA ** -0.5),
        "w_out": nrm(ks[13], (L, D_MODEL, D_MODEL), DN_BETA * D_MODEL ** -0.5),
        "ln1_g": 1.0 + nrm(ks[14], (L, D_MODEL), 0.01),
        "ln1_b": nrm(ks[15], (L, D_MODEL), 0.01),
        "w_up": nrm(ks[16], (L, D_MODEL, D_FF), DN_BETA * D_MODEL ** -0.5),
        "w_down": nrm(ks[17], (L, D_FF, D_MODEL), DN_BETA * D_FF ** -0.5),
        "ln2_g": 1.0 + nrm(ks[18], (L, D_MODEL), 0.01),
        "ln2_b": nrm(ks[19], (L, D_MODEL), 0.01),
    }


def reference(x_prompt, x_sample, w_in, w_pool, s_pool, sgu_ln_g, sgu_ln_b, w_s, b_s, rpb,
              w_br_pool, w_br_sgu, w_br_na, w_out, ln1_g, ln1_b, w_up, w_down, ln2_g, ln2_b):
    y_prompt = trunk(x_prompt, w_in, w_pool, s_pool, sgu_ln_g, sgu_ln_b, w_s, b_s, rpb,
                     w_br_pool, w_br_sgu, w_br_na, w_out, ln1_g, ln1_b, w_up, w_down, ln2_g, ln2_b)
    y_sample = trunk(x_sample, w_in, w_pool, s_pool, sgu_ln_g, sgu_ln_b, w_s, b_s, rpb,
                     w_br_pool, w_br_sgu, w_br_na, w_out, ln1_g, ln1_b, w_up, w_down, ln2_g, ln2_b)
    return (y_prompt, y_sample)
```

```python
import functools

import jax
import jax.numpy as jnp
import numpy as np
from jax import lax
from jax.experimental import pallas as pl
from jax.experimental.pallas import tpu as pltpu

F32 = jnp.float32
BF16 = jnp.bfloat16

GRID_W = 64
POOL_WINDOWS = (2, 4, 8, 16)
POOL_HALO = max(POOL_WINDOWS) // 2
CHUNK = 128
SGU_GROUPS = 8
NA_HEADS = 16
NA_HEAD_DIM = 64
NA_KH = 8
NA_KW = 16
N_BRANCH = 3
LN_EPS = 1e-5
NEG_BIG = -1e30

LANES = 128
V7X_VMEM_BYTES = 64 * 1024 * 1024
VMEM_COMPILER_RESERVE = 6 * 1024 * 1024

IN_TM = 1024
IN_TN = 1024
POOL_TM = 512
SGU_TM = 512
NA_Q_ROWS = 2
NA_BAND = NA_Q_ROWS + NA_KH - 1
NA_STEP_ROWS = 16
MERGE_TM = 1024
MERGE_TN = 512
OUT_TM = 512
FFN_TM = 512
FFN_TF = 512


def _params(semantics, vmem_estimate):
    limit = min(int(vmem_estimate) + VMEM_COMPILER_RESERVE, V7X_VMEM_BYTES - VMEM_COMPILER_RESERVE)
    return pltpu.CompilerParams(dimension_semantics=semantics, vmem_limit_bytes=limit)


def _layer_norm(y, g, b):
    mu = jnp.mean(y, axis=-1, keepdims=True)
    d = y - mu
    var = jnp.mean(d * d, axis=-1, keepdims=True)
    return d * lax.rsqrt(var + LN_EPS) * g + b


def _in_proj_kernel(x_ref, w_ref, g_ref, b_ref, a_ref, z_ref):
    j = pl.program_id(1)
    acc = jnp.dot(x_ref[...], w_ref[...], preferred_element_type=F32)

    @pl.when(j == 0)
    def _():
        a_ref[...] = acc

    @pl.when(j == 1)
    def _():
        z_ref[...] = jax.nn.gelu(acc).astype(BF16)

    @pl.when(j == 2)
    def _():
        z_ref[...] = _layer_norm(jax.nn.gelu(acc), g_ref[...], b_ref[...]).astype(BF16)

    @pl.when(j >= 3)
    def _():
        z_ref[...] = acc.astype(BF16)


def _in_proj(xb, w_in_b, sgu_g, sgu_b):
    n, d = xb.shape
    tm, tn = IN_TM, IN_TN
    nseg = 6
    vmem = 2 * (tm * d * 2 + d * tn * 2 + tm * tn * 4 + tm * tn * 2) + 2 * tm * tn * 4
    return pl.pallas_call(
        _in_proj_kernel,
        grid=(n // tm, nseg),
        in_specs=[
            pl.BlockSpec((tm, d), lambda i, j: (i, 0)),
            pl.BlockSpec((d, tn), lambda i, j: (0, j)),
            pl.BlockSpec((1, tn), lambda i, j: (0, 0)),
            pl.BlockSpec((1, tn), lambda i, j: (0, 0)),
        ],
        out_specs=[
            pl.BlockSpec((tm, tn), lambda i, j: (i, 0)),
            pl.BlockSpec((tm, tn), lambda i, j: (i, jnp.maximum(j - 1, 0))),
        ],
        out_shape=[
            jax.ShapeDtypeStruct((n, tn), F32),
            jax.ShapeDtypeStruct((n, (nseg - 1) * tn), BF16),
        ],
        compiler_params=_params(("parallel", "arbitrary"), vmem),
        name="in_proj",
    )(xb, w_in_b, sgu_g, sgu_b)


def _pool_kernel(ap_ref, ac_ref, an_ref, wp_ref, sp_ref, o_ref, buf_ref, *, seq, tm):
    i = pl.program_id(0)
    s0 = (i * tm) % seq
    h = POOL_HALO
    buf_ref[0:h, :] = jnp.where(s0 == 0, 0.0, ap_ref[...])
    buf_ref[h:h + tm, :] = ac_ref[...]
    buf_ref[h + tm:2 * h + tm, :] = jnp.where(s0 + tm == seq, 0.0, an_ref[...])
    pos = s0 + lax.broadcasted_iota(jnp.int32, (tm, 1), 0)
    gc = ac_ref.shape[1] // len(POOL_WINDOWS)
    for gi, w in enumerate(POOL_WINDOWS):
        cols = slice(gi * gc, (gi + 1) * gc)
        tot = buf_ref[h - w // 2:h - w // 2 + tm, cols]
        for dlt in range(-w // 2 + 1, w // 2):
            tot = tot + buf_ref[h + dlt:h + dlt + tm, cols]
        cnt = jnp.minimum(pos + w // 2, seq) - jnp.maximum(pos - w // 2, 0)
        p = tot * (1.0 / cnt.astype(F32)) - ac_ref[:, cols]
        y = jnp.dot(p.astype(BF16), wp_ref[gi], preferred_element_type=F32)
        o_ref[:, cols] = (y * sp_ref[:, cols]).astype(BF16)


def _pool(a, w_pool_b, s_pool, seq):
    n, dp = a.shape
    tm, h = POOL_TM, POOL_HALO
    assert seq % tm == 0 and tm % h == 0
    nb = tm // h
    last = n // h - 1
    vmem = 2 * (tm * dp * 4 + 2 * h * dp * 4 + tm * dp * 2) + w_pool_b.size * 4 + (tm + 2 * h) * dp * 4
    return pl.pallas_call(
        functools.partial(_pool_kernel, seq=seq, tm=tm),
        grid=(n // tm,),
        in_specs=[
            pl.BlockSpec((h, dp), lambda i: (jnp.maximum(i * nb - 1, 0), 0)),
            pl.BlockSpec((tm, dp), lambda i: (i, 0)),
            pl.BlockSpec((h, dp), lambda i: (jnp.minimum((i + 1) * nb, last), 0)),
            pl.BlockSpec(w_pool_b.shape, lambda i: (0, 0, 0)),
            pl.BlockSpec((1, dp), lambda i: (0, 0)),
        ],
        out_specs=pl.BlockSpec((tm, dp), lambda i: (i, 0)),
        out_shape=jax.ShapeDtypeStruct((n, dp), BF16),
        scratch_shapes=[pltpu.VMEM((tm + 2 * h, dp), F32)],
        compiler_params=_params(("parallel",), vmem),
        name="pool",
    )(a, a, a, w_pool_b, s_pool)


def _sgu_kernel(u_ref, v_ref, ws_ref, bs_ref, o_ref):
    tm, ds = u_ref.shape
    gc = ds // SGU_GROUPS
    for c in range(tm // CHUNK):
        rows = slice(c * CHUNK, (c + 1) * CHUNK)
        for g in range(SGU_GROUPS):
            cols = slice(g * gc, (g + 1) * gc)
            sg = jnp.dot(ws_ref[g], v_ref[rows, cols], preferred_element_type=F32) + bs_ref[g]
            o_ref[rows, cols] = (u_ref[rows, cols].astype(F32) * sg).astype(BF16)


def _sgu(zs, w_s_b, b_s_col):
    n = zs.shape[0]
    ds = SGU_GROUPS * w_s_b.shape[1]
    tm = SGU_TM
    vmem = 2 * (3 * tm * ds * 2) + 2 * (w_s_b.size * 2 + SGU_GROUPS * CHUNK * LANES * 4)
    return pl.pallas_call(
        _sgu_kernel,
        grid=(n // tm,),
        in_specs=[
            pl.BlockSpec((tm, ds), lambda i: (i, 0)),
            pl.BlockSpec((tm, ds), lambda i: (i, 1)),
            pl.BlockSpec(w_s_b.shape, lambda i: (0, 0, 0)),
            pl.BlockSpec(b_s_col.shape, lambda i: (0, 0, 0)),
        ],
        out_specs=pl.BlockSpec((tm, ds), lambda i: (i, 0)),
        out_shape=jax.ShapeDtypeStruct((n, ds), BF16),
        compiler_params=_params(("parallel",), vmem),
        name="sgu",
    )(zs, zs, w_s_b, b_s_col)


def _na_cases():
    assert (NA_KH // 2) % NA_Q_ROWS == 0
    return (NA_KH // 2) // NA_Q_ROWS


def _na_bias_table(rpb_l):
    n_edge = _na_cases()
    rows = 2 * NA_KH + NA_Q_ROWS * (2 * n_edge + 1)
    n_tiles = rows // NA_Q_ROWS
    tiles = list(range(n_edge)) + [n_edge] + list(range(n_tiles - n_edge, n_tiles))
    tq, tk = NA_Q_ROWS * GRID_W, NA_BAND * GRID_W
    dr_idx = np.zeros((len(tiles), tq, tk), np.int32)
    dc_idx = np.zeros((len(tiles), tq, tk), np.int32)
    valid = np.zeros((len(tiles), tq, tk), bool)
    qi = np.arange(tq)
    kj = np.arange(tk)
    for ci, rt in enumerate(tiles):
        rs = rt * NA_Q_ROWS
        b0 = int(np.clip(rs - NA_KH // 2, 0, rows - NA_BAND))
        qrow = rs + qi // GRID_W
        qcol = qi % GRID_W
        krow = b0 + kj // GRID_W
        kcol = kj % GRID_W
        r0 = np.clip(qrow - NA_KH // 2, 0, rows - NA_KH)
        c0 = np.clip(qcol - NA_KW // 2, 0, GRID_W - NA_KW)
        ok = ((krow[None, :] >= r0[:, None]) & (krow[None, :] < r0[:, None] + NA_KH)
              & (kcol[None, :] >= c0[:, None]) & (kcol[None, :] < c0[:, None] + NA_KW))
        dr = krow[None, :] - qrow[:, None] + (NA_KH - 1)
        dc = kcol[None, :] - qcol[:, None] + (NA_KW - 1)
        valid[ci] = ok
        dr_idx[ci] = np.where(ok, dr, 0)
        dc_idx[ci] = np.where(ok, dc, 0)
    tab = rpb_l[:, dr_idx, dc_idx]
    tab = jnp.where(valid[None], tab, NEG_BIG)
    return tab.transpose(1, 0, 2, 3)


def _na_kernel(q_ref, k_ref, v_ref, bias_ref, o_ref, *, rows):
    tq, tk = NA_Q_ROWS * GRID_W, NA_BAND * GRID_W
    n_sub = NA_STEP_ROWS // NA_Q_ROWS
    n_edge = _na_cases()
    n_tiles = rows // NA_Q_ROWS
    t0 = pl.program_id(2) * n_sub
    lane = lax.broadcasted_iota(jnp.int32, (1, LANES), 1)
    head_lanes = [lane < NA_HEAD_DIM, lane >= NA_HEAD_DIM]

    def body(t, carry):
        rt = t0 + t
        rs = rt * NA_Q_ROWS
        b0 = jnp.clip(rs - NA_KH // 2, 0, rows - NA_BAND)
        koff = pl.multiple_of(b0 * GRID_W, GRID_W)
        qoff = pl.multiple_of(t * tq, tq)
        kb = k_ref[pl.ds(koff, tk), :]
        vb = v_ref[pl.ds(koff, tk), :]
        q = q_ref[pl.ds(qoff, tq), :]
        case = jnp.where(rt < n_edge, rt,
                         jnp.where(rt >= n_tiles - n_edge, rt - (n_tiles - n_edge) + n_edge + 1, n_edge))
        outs = []
        for hh in range(2):
            qh = jnp.where(head_lanes[hh], q, jnp.zeros_like(q)) * (NA_HEAD_DIM ** -0.5)
            s = lax.dot_general(qh, kb, (((1,), (1,)), ((), ())), preferred_element_type=F32)
            s = s + bias_ref[case, hh]
            m = jnp.max(s, axis=-1, keepdims=True)
            e = jnp.exp(s - m)
            denom = jnp.sum(e, axis=-1, keepdims=True)
            o = jnp.dot(e.astype(BF16), vb, preferred_element_type=F32)
            outs.append(o * (1.0 / denom))
        o_ref[pl.ds(qoff, tq), :] = jnp.where(head_lanes[0], outs[0], outs[1]).astype(BF16)
        return carry

    lax.fori_loop(0, n_sub, body, 0)


def _natten(zs, bias_tab, batch, seq):
    n = zs.shape[0]
    rows = seq // GRID_W
    assert rows % NA_STEP_ROWS == 0 and rows >= 2 * NA_KH + NA_STEP_ROWS
    n_pairs = NA_HEADS * NA_HEAD_DIM // LANES
    d_na = NA_HEADS * NA_HEAD_DIM
    cb = d_na // LANES
    steps = rows // NA_STEP_ROWS
    tqs = NA_STEP_ROWS * GRID_W
    tq, tk = NA_Q_ROWS * GRID_W, NA_BAND * GRID_W
    n_case = bias_tab.shape[0]
    vmem = 2 * (2 * tqs * LANES * 2 + 2 * seq * LANES * 2 + n_case * 2 * tq * tk * 4) + 8 * tq * tk * 4
    return pl.pallas_call(
        functools.partial(_na_kernel, rows=rows),
        grid=(batch, n_pairs, steps),
        in_specs=[
            pl.BlockSpec((tqs, LANES), lambda b, hp, r: (b * steps + r, 2 * cb + hp)),
            pl.BlockSpec((seq, LANES), lambda b, hp, r: (b, 3 * cb + hp)),
            pl.BlockSpec((seq, LANES), lambda b, hp, r: (b, 4 * cb + hp)),
            pl.BlockSpec((n_case, 2, tq, tk), lambda b, hp, r: (0, hp, 0, 0)),
        ],
        out_specs=pl.BlockSpec((tqs, LANES), lambda b, hp, r: (b * steps + r, hp)),
        out_shape=jax.ShapeDtypeStruct((n, d_na), BF16),
        compiler_params=_params(("parallel", "parallel", "arbitrary"), vmem),
        name="natten",
    )(zs, zs, zs, bias_tab)


def _merge_kernel(x_ref, pa_ref, pb_ref, pc_ref, wg0_ref, wg1_ref, wg2_ref, wa_ref, wb_ref, wc_ref, o_ref):
    x = x_ref[...]
    acc = None
    for br_ref, wg_ref, wbr_ref in ((pa_ref, wg0_ref, wa_ref), (pb_ref, wg1_ref, wb_ref), (pc_ref, wg2_ref, wc_ref)):
        gate = jax.nn.sigmoid(jnp.dot(x, wg_ref[...], preferred_element_type=F32))
        y = jnp.dot(br_ref[...], wbr_ref[...], preferred_element_type=F32)
        acc = gate * y if acc is None else acc + gate * y
    o_ref[...] = acc.astype(BF16)


def _merge(xb, pa, pb, pc, w_in_b, w_a, w_b, w_c):
    n, d = xb.shape
    dbr = pa.shape[1]
    tm, tn = MERGE_TM, MERGE_TN
    gate0 = (w_in_b.shape[1] - N_BRANCH * d) // tn
    per = d // tn
    vmem = 2 * (tm * d * 2 + 3 * tm * dbr * 2 + 3 * d * tn * 2 + 3 * dbr * tn * 2 + tm * tn * 2) + 4 * tm * tn * 4
    gate_spec = lambda b: pl.BlockSpec((d, tn), lambda i, j: (0, gate0 + b * per + j))
    br_spec = pl.BlockSpec((tm, dbr), lambda i, j: (i, 0))
    wbr_spec = pl.BlockSpec((dbr, tn), lambda i, j: (0, j))
    return pl.pallas_call(
        _merge_kernel,
        grid=(n // tm, per),
        in_specs=[pl.BlockSpec((tm, d), lambda i, j: (i, 0)), br_spec, br_spec, br_spec,
                  gate_spec(0), gate_spec(1), gate_spec(2), wbr_spec, wbr_spec, wbr_spec],
        out_specs=pl.BlockSpec((tm, tn), lambda i, j: (i, j)),
        out_shape=jax.ShapeDtypeStruct((n, d), BF16),
        compiler_params=_params(("parallel", "arbitrary"), vmem),
        name="merge",
    )(xb, pa, pb, pc, w_in_b, w_in_b, w_in_b, w_a, w_b, w_c)


def _out_ln_kernel(m_ref, x_ref, w_ref, g_ref, b_ref, of_ref, ob_ref, *, alpha):
    y = alpha * x_ref[...] + jnp.dot(m_ref[...], w_ref[...], preferred_element_type=F32)
    y = _layer_norm(y, g_ref[...], b_ref[...])
    of_ref[...] = y
    ob_ref[...] = y.astype(BF16)


def _out_ln(merged, xf, w_out_b, g, b, alpha):
    n, d = xf.shape
    tm = OUT_TM
    vmem = 2 * (tm * d * 2 + tm * d * 4 + d * d * 2 + tm * d * 4 + tm * d * 2) + 2 * tm * d * 4
    row = pl.BlockSpec((tm, d), lambda i: (i, 0))
    vec = pl.BlockSpec((1, d), lambda i: (0, 0))
    return pl.pallas_call(
        functools.partial(_out_ln_kernel, alpha=alpha),
        grid=(n // tm,),
        in_specs=[row, row, pl.BlockSpec((d, d), lambda i: (0, 0)), vec, vec],
        out_specs=[row, row],
        out_shape=[jax.ShapeDtypeStruct((n, d), F32), jax.ShapeDtypeStruct((n, d), BF16)],
        compiler_params=_params(("parallel",), vmem),
        name="out_ln",
    )(merged, xf, w_out_b, g, b)


def _ffn_kernel(xb_ref, xf_ref, wu_ref, wd_ref, g_ref, b_ref, of_ref, ob_ref, *, alpha):
    f = pl.program_id(1)

    @pl.when(f == 0)
    def _():
        of_ref[...] = alpha * xf_ref[...]

    h = jnp.maximum(jnp.dot(xb_ref[...], wu_ref[...], preferred_element_type=F32), 0.0)
    of_ref[...] += jnp.dot((h * h).astype(BF16), wd_ref[...], preferred_element_type=F32)

    @pl.when(f == pl.num_programs(1) - 1)
    def _():
        y = _layer_norm(of_ref[...], g_ref[...], b_ref[...])
        of_ref[...] = y
        ob_ref[...] = y.astype(BF16)


def _ffn(xb, xf, w_up_b, w_down_b, g, b, alpha):
    n, d = xf.shape
    dff = w_up_b.shape[1]
    tm, tf = FFN_TM, FFN_TF
    vmem = 2 * (tm * d * 2 + tm * d * 4 + d * tf * 2 + tf * d * 2 + tm * d * 4 + tm * d * 2) + 3 * tm * tf * 4
    row = pl.BlockSpec((tm, d), lambda i, f: (i, 0))
    vec = pl.BlockSpec((1, d), lambda i, f: (0, 0))
    return pl.pallas_call(
        functools.partial(_ffn_kernel, alpha=alpha),
        grid=(n // tm, dff // tf),
        in_specs=[row, row, pl.BlockSpec((d, tf), lambda i, f: (0, f)), pl.BlockSpec((tf, d), lambda i, f: (f, 0)),
                  vec, vec],
        out_specs=[row, row],
        out_shape=[jax.ShapeDtypeStruct((n, d), F32), jax.ShapeDtypeStruct((n, d), BF16)],
        compiler_params=_params(("parallel", "arbitrary"), vmem),
        name="ffn",
    )(xb, xf, w_up_b, w_down_b, g, b)


def kernel(x_prompt, x_sample, w_in, w_pool, s_pool, sgu_ln_g, sgu_ln_b, w_s, b_s, rpb, w_br_pool, w_br_sgu,
           w_br_na, w_out, ln1_g, ln1_b, w_up, w_down, ln2_g, ln2_b):
    depth = w_in.shape[0]
    alpha = float((2 * depth) ** 0.25)
    w_in_b, w_pool_b, w_s_b = w_in.astype(BF16), w_pool.astype(BF16), w_s.astype(BF16)
    w_a, w_b, w_c = w_br_pool.astype(BF16), w_br_sgu.astype(BF16), w_br_na.astype(BF16)
    w_out_b, w_up_b, w_down_b = w_out.astype(BF16), w_up.astype(BF16), w_down.astype(BF16)
    row = lambda v: v[None, :]
    bias_tabs = [_na_bias_table(rpb[l]) for l in range(depth)]

    outs = []
    for x in (x_prompt, x_sample):
        batch, seq, d = x.shape
        xf = x.reshape(batch * seq, d)
        xb = xf.astype(BF16)
        for l in range(depth):
            a, zs = _in_proj(xb, w_in_b[l], row(sgu_ln_g[l]), row(sgu_ln_b[l]))
            pa = _pool(a, w_pool_b[l], row(s_pool[l]), seq)
            pb = _sgu(zs, w_s_b[l], b_s[l][:, :, None])
            pc = _natten(zs, bias_tabs[l], batch, seq)
            merged = _merge(xb, pa, pb, pc, w_in_b[l], w_a[l], w_b[l], w_c[l])
            xf, xb = _out_ln(merged, xf, w_out_b[l], row(ln1_g[l]), row(ln1_b[l]), alpha)
            xf, xb = _ffn(xb, xf, w_up_b[l], w_down_b[l], row(ln2_g[l]), row(ln2_b[l]), alpha)
        outs.append(xf.reshape(batch, seq, d))
    return tuple(outs)
```

```python
import functools

import jax
import jax.numpy as jnp
import numpy as np
from jax import lax
from jax.experimental import pallas as pl
from jax.experimental.pallas import tpu as pltpu

F32 = jnp.float32
BF16 = jnp.bfloat16

GRID_W = 64
POOL_WINDOWS = (2, 4, 8, 16)
POOL_HALO = max(POOL_WINDOWS) // 2
CHUNK = 128
SGU_GROUPS = 8
NA_HEADS = 16
NA_HEAD_DIM = 64
NA_KH = 8
NA_KW = 16
N_BRANCH = 3
LN_EPS = 1e-5
NEG_BIG = -1e30

LANES = 128
V7X_VMEM_BYTES = 64 * 1024 * 1024
VMEM_COMPILER_RESERVE = 6 * 1024 * 1024

PROJ_TM = 1024
PROJ_TN = 1024
PROJ_ROWS = 256
POOL_TM = 512
SGU_TM = 512
NA_Q_ROWS = 2
NA_BAND = NA_Q_ROWS + NA_KH - 1
NA_STEP_ROWS = 16
MERGE_TM = 1024
MERGE_TN = 512
OUT_TM = 512
OUT_ROWS = 128
FFN_TM = 512
FFN_TF = 1024


def _params(semantics, vmem_estimate):
    limit = min(int(vmem_estimate) + VMEM_COMPILER_RESERVE, V7X_VMEM_BYTES - VMEM_COMPILER_RESERVE)
    return pltpu.CompilerParams(dimension_semantics=semantics, vmem_limit_bytes=limit)


def _layer_norm(y, g, b):
    mu = jnp.mean(y, axis=-1, keepdims=True)
    d = y - mu
    var = jnp.mean(d * d, axis=-1, keepdims=True)
    return d * lax.rsqrt(var + LN_EPS) * g + b


def _proj_kernel(x_ref, w_ref, g_ref, b_ref, o_ref, *, mode):
    for r in range(x_ref.shape[0] // PROJ_ROWS):
        rows = slice(r * PROJ_ROWS, (r + 1) * PROJ_ROWS)
        acc = jnp.dot(x_ref[rows, :], w_ref[...], preferred_element_type=F32)
        if mode == "gelu":
            acc = jax.nn.gelu(acc)
        elif mode == "gelu_ln":
            acc = _layer_norm(jax.nn.gelu(acc), g_ref[...], b_ref[...])
        o_ref[rows, :] = acc.astype(o_ref.dtype)


def _proj(xb, w_in_b, seg0, nseg, mode, out_dtype, ln_g, ln_b):
    n, d = xb.shape
    tm, tn = PROJ_TM, PROJ_TN
    osz = jnp.dtype(out_dtype).itemsize
    vmem = 2 * (tm * d * 2 + d * tn * 2 + tm * tn * osz) + 4 * PROJ_ROWS * tn * 4
    return pl.pallas_call(
        functools.partial(_proj_kernel, mode=mode),
        grid=(n // tm, nseg),
        in_specs=[
            pl.BlockSpec((tm, d), lambda i, j: (i, 0)),
            pl.BlockSpec((d, tn), lambda i, j: (0, seg0 + j)),
            pl.BlockSpec((1, tn), lambda i, j: (0, 0)),
            pl.BlockSpec((1, tn), lambda i, j: (0, 0)),
        ],
        out_specs=pl.BlockSpec((tm, tn), lambda i, j: (i, j)),
        out_shape=jax.ShapeDtypeStruct((n, nseg * tn), out_dtype),
        compiler_params=_params(("parallel", "arbitrary"), vmem),
        name="proj%d_%s" % (seg0, mode),
    )(xb, w_in_b, ln_g, ln_b)


def _pool_kernel(ap_ref, ac_ref, an_ref, wp_ref, sp_ref, o_ref, buf_ref, *, seq, tm):
    i = pl.program_id(0)
    s0 = (i * tm) % seq
    h = POOL_HALO
    buf_ref[0:h, :] = jnp.where(s0 == 0, 0.0, ap_ref[...])
    buf_ref[h:h + tm, :] = ac_ref[...]
    buf_ref[h + tm:2 * h + tm, :] = jnp.where(s0 + tm == seq, 0.0, an_ref[...])
    pos = s0 + lax.broadcasted_iota(jnp.int32, (tm, 1), 0)
    gc = ac_ref.shape[1] // len(POOL_WINDOWS)
    for gi, w in enumerate(POOL_WINDOWS):
        cols = slice(gi * gc, (gi + 1) * gc)
        tot = buf_ref[h - w // 2:h - w // 2 + tm, cols]
        for dlt in range(-w // 2 + 1, w // 2):
            tot = tot + buf_ref[h + dlt:h + dlt + tm, cols]
        cnt = jnp.minimum(pos + w // 2, seq) - jnp.maximum(pos - w // 2, 0)
        p = tot * (1.0 / cnt.astype(F32)) - ac_ref[:, cols]
        y = jnp.dot(p.astype(BF16), wp_ref[gi], preferred_element_type=F32)
        o_ref[:, cols] = (y * sp_ref[:, cols]).astype(BF16)


def _pool(a, w_pool_b, s_pool, seq):
    n, dp = a.shape
    tm, h = POOL_TM, POOL_HALO
    assert seq % tm == 0 and tm % h == 0
    nb = tm // h
    last = n // h - 1
    vmem = 2 * (tm * dp * 4 + 2 * h * dp * 4 + tm * dp * 2) + w_pool_b.size * 4 + (tm + 2 * h) * dp * 4
    return pl.pallas_call(
        functools.partial(_pool_kernel, seq=seq, tm=tm),
        grid=(n // tm,),
        in_specs=[
            pl.BlockSpec((h, dp), lambda i: (jnp.maximum(i * nb - 1, 0), 0)),
            pl.BlockSpec((tm, dp), lambda i: (i, 0)),
            pl.BlockSpec((h, dp), lambda i: (jnp.minimum((i + 1) * nb, last), 0)),
            pl.BlockSpec(w_pool_b.shape, lambda i: (0, 0, 0)),
            pl.BlockSpec((1, dp), lambda i: (0, 0)),
        ],
        out_specs=pl.BlockSpec((tm, dp), lambda i: (i, 0)),
        out_shape=jax.ShapeDtypeStruct((n, dp), BF16),
        scratch_shapes=[pltpu.VMEM((tm + 2 * h, dp), F32)],
        compiler_params=_params(("parallel",), vmem),
        name="pool",
    )(a, a, a, w_pool_b, s_pool)


def _sgu_kernel(u_ref, v_ref, ws_ref, bs_ref, o_ref):
    tm, ds = u_ref.shape
    gc = ds // SGU_GROUPS
    for c in range(tm // CHUNK):
        rows = slice(c * CHUNK, (c + 1) * CHUNK)
        for g in range(SGU_GROUPS):
            cols = slice(g * gc, (g + 1) * gc)
            sg = jnp.dot(ws_ref[g], v_ref[rows, cols], preferred_element_type=F32) + bs_ref[g]
            o_ref[rows, cols] = (u_ref[rows, cols].astype(F32) * sg).astype(BF16)


def _sgu(ug, vn, w_s_b, b_s_col):
    n, ds = ug.shape
    tm = SGU_TM
    vmem = 2 * (3 * tm * ds * 2) + 2 * (w_s_b.size * 2 + SGU_GROUPS * CHUNK * LANES * 4)
    row = pl.BlockSpec((tm, ds), lambda i: (i, 0))
    return pl.pallas_call(
        _sgu_kernel,
        grid=(n // tm,),
        in_specs=[row, row,
                  pl.BlockSpec(w_s_b.shape, lambda i: (0, 0, 0)),
                  pl.BlockSpec(b_s_col.shape, lambda i: (0, 0, 0))],
        out_specs=row,
        out_shape=jax.ShapeDtypeStruct((n, ds), BF16),
        compiler_params=_params(("parallel",), vmem),
        name="sgu",
    )(ug, vn, w_s_b, b_s_col)


def _na_edge_tiles():
    assert (NA_KH // 2) % NA_Q_ROWS == 0
    return (NA_KH // 2) // NA_Q_ROWS


def _na_bias_table(rpb_l):
    n_edge = _na_edge_tiles()
    rows = 2 * NA_KH + NA_Q_ROWS * (2 * n_edge + 1)
    n_tiles = rows // NA_Q_ROWS
    tiles = list(range(n_edge)) + [n_edge] + list(range(n_tiles - n_edge, n_tiles))
    n_case = len(tiles)
    c = np.arange(GRID_W)
    c0 = np.clip(c - NA_KW // 2, 0, GRID_W - NA_KW)
    col_ok = (c[None, :] >= c0[:, None]) & (c[None, :] < c0[:, None] + NA_KW)
    dc = c[None, :] - c[:, None] + (NA_KW - 1)
    onehot = ((dc[None] == np.arange(2 * NA_KW - 1)[:, None, None]) & col_ok[None]).astype(np.float32)
    toep = jnp.einsum("hrd,dck->hrck", rpb_l, onehot, precision=lax.Precision.HIGHEST)
    toep = jnp.where(col_ok[None, None], toep, NEG_BIG)
    dr_idx = np.zeros((n_case, NA_Q_ROWS, NA_BAND), np.int32)
    row_ok = np.zeros((n_case, NA_Q_ROWS, NA_BAND), bool)
    for ci, rt in enumerate(tiles):
        rs = rt * NA_Q_ROWS
        b0 = int(np.clip(rs - NA_KH // 2, 0, rows - NA_BAND))
        qrow = rs + np.arange(NA_Q_ROWS)
        krow = b0 + np.arange(NA_BAND)
        r0 = np.clip(qrow - NA_KH // 2, 0, rows - NA_KH)
        ok = (krow[None, :] >= r0[:, None]) & (krow[None, :] < r0[:, None] + NA_KH)
        row_ok[ci] = ok
        dr_idx[ci] = np.where(ok, krow[None, :] - qrow[:, None] + (NA_KH - 1), 0)
    tab = jnp.take(toep, dr_idx.reshape(-1), axis=1)
    tab = tab.reshape(NA_HEADS, n_case, NA_Q_ROWS, NA_BAND, GRID_W, GRID_W)
    tab = jnp.where(row_ok[None, :, :, :, None, None], tab, NEG_BIG)
    tab = tab.transpose(1, 0, 2, 4, 3, 5)
    return tab.reshape(n_case, NA_HEADS // 2, 2 * NA_Q_ROWS * GRID_W, NA_BAND * GRID_W)


def _na_kernel(q_ref, k_ref, v_ref, bias_ref, o_ref, *, rows):
    tq, tk = NA_Q_ROWS * GRID_W, NA_BAND * GRID_W
    n_sub = NA_STEP_ROWS // NA_Q_ROWS
    n_edge = _na_edge_tiles()
    n_tiles = rows // NA_Q_ROWS
    t0 = pl.program_id(2) * n_sub
    lane = lax.broadcasted_iota(jnp.int32, (1, LANES), 1)
    first_head = lane < NA_HEAD_DIM

    for t in range(n_sub):
        rt = t0 + t
        b0 = jnp.clip(rt * NA_Q_ROWS - NA_KH // 2, 0, rows - NA_BAND)
        koff = pl.multiple_of(b0 * GRID_W, GRID_W)
        kb = k_ref[pl.ds(koff, tk), :]
        vb = v_ref[pl.ds(koff, tk), :]
        q = q_ref[t * tq:(t + 1) * tq, :]
        case = jnp.where(rt < n_edge, rt,
                         jnp.where(rt >= n_tiles - n_edge, rt - (n_tiles - n_edge) + n_edge + 1, n_edge))
        zero = jnp.zeros_like(q)
        q2 = jnp.concatenate([jnp.where(first_head, q, zero), jnp.where(first_head, zero, q)], axis=0)
        q2 = q2 * (NA_HEAD_DIM ** -0.5)
        s = lax.dot_general(q2, kb, (((1,), (1,)), ((), ())), preferred_element_type=F32)
        s = s + bias_ref[case]
        m = jnp.max(s, axis=-1, keepdims=True)
        e = jnp.exp(s - m)
        denom = jnp.sum(e, axis=-1, keepdims=True)
        o2 = jnp.dot(e.astype(BF16), vb, preferred_element_type=F32) * (1.0 / denom)
        o_ref[t * tq:(t + 1) * tq, :] = jnp.where(first_head, o2[:tq], o2[tq:]).astype(BF16)


def _natten(qkv, bias_tab, batch, seq):
    n = qkv.shape[0]
    rows = seq // GRID_W
    assert rows % NA_STEP_ROWS == 0 and rows >= 2 * NA_KH + NA_STEP_ROWS
    d_na = NA_HEADS * NA_HEAD_DIM
    n_pairs = d_na // LANES
    steps = rows // NA_STEP_ROWS
    tqs = NA_STEP_ROWS * GRID_W
    tq, tk = NA_Q_ROWS * GRID_W, NA_BAND * GRID_W
    n_case = bias_tab.shape[0]
    vmem = 2 * (2 * tqs * LANES * 2 + 2 * seq * LANES * 2 + n_case * 2 * tq * tk * 4) + 24 * tq * tk * 4
    return pl.pallas_call(
        functools.partial(_na_kernel, rows=rows),
        grid=(batch, n_pairs, steps),
        in_specs=[
            pl.BlockSpec((tqs, LANES), lambda b, hp, r: (b * steps + r, hp)),
            pl.BlockSpec((seq, LANES), lambda b, hp, r: (b, n_pairs + hp)),
            pl.BlockSpec((seq, LANES), lambda b, hp, r: (b, 2 * n_pairs + hp)),
            pl.BlockSpec((n_case, None, 2 * tq, tk), lambda b, hp, r: (0, hp, 0, 0)),
        ],
        out_specs=pl.BlockSpec((tqs, LANES), lambda b, hp, r: (b * steps + r, hp)),
        out_shape=jax.ShapeDtypeStruct((n, d_na), BF16),
        compiler_params=_params(("parallel", "parallel", "arbitrary"), vmem),
        name="natten",
    )(qkv, qkv, qkv, bias_tab)


def _merge_kernel(x_ref, pa_ref, pb_ref, pc_ref, wg0_ref, wg1_ref, wg2_ref, wa_ref, wb_ref, wc_ref, o_ref):
    x = x_ref[...]
    acc = None
    for br_ref, wg_ref, wbr_ref in ((pa_ref, wg0_ref, wa_ref), (pb_ref, wg1_ref, wb_ref), (pc_ref, wg2_ref, wc_ref)):
        gate = jax.nn.sigmoid(jnp.dot(x, wg_ref[...], preferred_element_type=F32))
        y = jnp.dot(br_ref[...], wbr_ref[...], preferred_element_type=F32)
        acc = gate * y if acc is None else acc + gate * y
    o_ref[...] = acc.astype(BF16)


def _merge(xb, pa, pb, pc, w_in_b, w_a, w_b, w_c):
    n, d = xb.shape
    dbr = pa.shape[1]
    tm, tn = MERGE_TM, MERGE_TN
    gate0 = (w_in_b.shape[1] - N_BRANCH * d) // tn
    per = d // tn
    vmem = 2 * (tm * d * 2 + 3 * tm * dbr * 2 + 3 * d * tn * 2 + 3 * dbr * tn * 2 + tm * tn * 2) + 4 * tm * tn * 4
    gate_spec = lambda b: pl.BlockSpec((d, tn), lambda i, j: (0, gate0 + b * per + j))
    br_spec = pl.BlockSpec((tm, dbr), lambda i, j: (i, 0))
    wbr_spec = pl.BlockSpec((dbr, tn), lambda i, j: (0, j))
    return pl.pallas_call(
        _merge_kernel,
        grid=(n // tm, per),
        in_specs=[pl.BlockSpec((tm, d), lambda i, j: (i, 0)), br_spec, br_spec, br_spec,
                  gate_spec(0), gate_spec(1), gate_spec(2), wbr_spec, wbr_spec, wbr_spec],
        out_specs=pl.BlockSpec((tm, tn), lambda i, j: (i, j)),
        out_shape=jax.ShapeDtypeStruct((n, d), BF16),
        compiler_params=_params(("parallel", "arbitrary"), vmem),
        name="merge",
    )(xb, pa, pb, pc, w_in_b, w_in_b, w_in_b, w_a, w_b, w_c)


def _out_ln_kernel(m_ref, x_ref, w_ref, g_ref, b_ref, of_ref, ob_ref, *, alpha):
    for r in range(m_ref.shape[0] // OUT_ROWS):
        rows = slice(r * OUT_ROWS, (r + 1) * OUT_ROWS)
        y = alpha * x_ref[rows, :] + jnp.dot(m_ref[rows, :], w_ref[...], preferred_element_type=F32)
        y = _layer_norm(y, g_ref[...], b_ref[...])
        of_ref[rows, :] = y
        ob_ref[rows, :] = y.astype(BF16)


def _out_ln(merged, xf, w_out_b, g, b, alpha):
    n, d = xf.shape
    tm = OUT_TM
    vmem = 2 * (tm * d * 2 + tm * d * 4 + d * d * 2 + tm * d * 4 + tm * d * 2) + 4 * OUT_ROWS * d * 4
    row = pl.BlockSpec((tm, d), lambda i: (i, 0))
    vec = pl.BlockSpec((1, d), lambda i: (0, 0))
    return pl.pallas_call(
        functools.partial(_out_ln_kernel, alpha=alpha),
        grid=(n // tm,),
        in_specs=[row, row, pl.BlockSpec((d, d), lambda i: (0, 0)), vec, vec],
        out_specs=[row, row],
        out_shape=[jax.ShapeDtypeStruct((n, d), F32), jax.ShapeDtypeStruct((n, d), BF16)],
        compiler_params=_params(("parallel",), vmem),
        name="out_ln",
    )(merged, xf, w_out_b, g, b)


def _ffn_kernel(xb_ref, xf_ref, wu_ref, wd_ref, g_ref, b_ref, of_ref, ob_ref, *, alpha):
    f = pl.program_id(1)

    @pl.when(f == 0)
    def _():
        of_ref[...] = alpha * xf_ref[...]

    h = jnp.maximum(jnp.dot(xb_ref[...], wu_ref[...], preferred_element_type=F32), 0.0)
    of_ref[...] += jnp.dot((h * h).astype(BF16), wd_ref[...], preferred_element_type=F32)

    @pl.when(f == pl.num_programs(1) - 1)
    def _():
        y = _layer_norm(of_ref[...], g_ref[...], b_ref[...])
        of_ref[...] = y
        ob_ref[...] = y.astype(BF16)


def _ffn(xb, xf, w_up_b, w_down_b, g, b, alpha):
    n, d = xf.shape
    dff = w_up_b.shape[1]
    tm, tf = FFN_TM, FFN_TF
    vmem = 2 * (tm * d * 2 + tm * d * 4 + d * tf * 2 + tf * d * 2 + tm * d * 4 + tm * d * 2) + 3 * tm * tf * 4
    row = pl.BlockSpec((tm, d), lambda i, f: (i, 0))
    vec = pl.BlockSpec((1, d), lambda i, f: (0, 0))
    return pl.pallas_call(
        functools.partial(_ffn_kernel, alpha=alpha),
        grid=(n // tm, dff // tf),
        in_specs=[row, row, pl.BlockSpec((d, tf), lambda i, f: (0, f)), pl.BlockSpec((tf, d), lambda i, f: (f, 0)),
                  vec, vec],
        out_specs=[row, row],
        out_shape=[jax.ShapeDtypeStruct((n, d), F32), jax.ShapeDtypeStruct((n, d), BF16)],
        compiler_params=_params(("parallel", "arbitrary"), vmem),
        name="ffn",
    )(xb, xf, w_up_b, w_down_b, g, b)


def kernel(x_prompt, x_sample, w_in, w_pool, s_pool, sgu_ln_g, sgu_ln_b, w_s, b_s, rpb, w_br_pool, w_br_sgu,
           w_br_na, w_out, ln1_g, ln1_b, w_up, w_down, ln2_g, ln2_b):
    depth = w_in.shape[0]
    alpha = float((2 * depth) ** 0.25)
    w_in_b, w_pool_b, w_s_b = w_in.astype(BF16), w_pool.astype(BF16), w_s.astype(BF16)
    w_a, w_b, w_c = w_br_pool.astype(BF16), w_br_sgu.astype(BF16), w_br_na.astype(BF16)
    w_out_b, w_up_b, w_down_b = w_out.astype(BF16), w_up.astype(BF16), w_down.astype(BF16)
    row = lambda v: v[None, :]
    bias_tabs = [_na_bias_table(rpb[l]) for l in range(depth)]

    outs = []
    for x in (x_prompt, x_sample):
        batch, seq, d = x.shape
        xf = x.reshape(batch * seq, d)
        xb = xf.astype(BF16)
        for l in range(depth):
            ln = (row(sgu_ln_g[l]), row(sgu_ln_b[l]))
            a = _proj(xb, w_in_b[l], 0, 1, "plain", F32, *ln)
            ug = _proj(xb, w_in_b[l], 1, 1, "gelu", BF16, *ln)
            vn = _proj(xb, w_in_b[l], 2, 1, "gelu_ln", BF16, *ln)
            qkv = _proj(xb, w_in_b[l], 3, 3, "plain", BF16, *ln)
            pa = _pool(a, w_pool_b[l], row(s_pool[l]), seq)
            pb = _sgu(ug, vn, w_s_b[l], b_s[l][:, :, None])
            pc = _natten(qkv, bias_tabs[l], batch, seq)
            merged = _merge(xb, pa, pb, pc, w_in_b[l], w_a[l], w_b[l], w_c[l])
            xf, xb = _out_ln(merged, xf, w_out_b[l], row(ln1_g[l]), row(ln1_b[l]), alpha)
            xf, xb = _ffn(xb, xf, w_up_b[l], w_down_b[l], row(ln2_g[l]), row(ln2_b[l]), alpha)
        outs.append(xf.reshape(batch, seq, d))
    return tuple(outs)
```

```python
import functools

import jax
import jax.numpy as jnp
import numpy as np
from jax import lax
from jax.experimental import pallas as pl
from jax.experimental.pallas import tpu as pltpu

F32 = jnp.float32
BF16 = jnp.bfloat16

GRID_W = 64
POOL_WINDOWS = (2, 4, 8, 16)
POOL_HALO = max(POOL_WINDOWS) // 2
CHUNK = 128
SGU_GROUPS = 8
NA_HEADS = 16
NA_HEAD_DIM = 64
NA_KH = 8
NA_KW = 16
N_BRANCH = 3
LN_EPS = 1e-5
NEG_BIG = -1e30

LANES = 128
V7X_VMEM_BYTES = 64 * 1024 * 1024
VMEM_COMPILER_RESERVE = 6 * 1024 * 1024

PROJ_TM = 2048
PROJ_TN = 1024
PROJ_ROWS = 256
POOL_TM = 512
SGU_TM = 512
NA_Q_ROWS = 4
NA_BAND = NA_Q_ROWS + NA_KH - 1
NA_STEP_ROWS = 32
MERGE_TM = 1024
MERGE_TN = 512
OUT_TM = 512
OUT_ROWS = 128
FFN_TM = 512
FFN_TF = 1024


def _params(semantics, vmem_estimate):
    limit = min(int(vmem_estimate) + VMEM_COMPILER_RESERVE, V7X_VMEM_BYTES - VMEM_COMPILER_RESERVE)
    return pltpu.CompilerParams(dimension_semantics=semantics, vmem_limit_bytes=limit)


def _layer_norm(y, g, b):
    mu = jnp.mean(y, axis=-1, keepdims=True)
    d = y - mu
    var = jnp.mean(d * d, axis=-1, keepdims=True)
    return d * lax.rsqrt(var + LN_EPS) * g + b


def _proj_kernel(x_ref, w_ref, g_ref, b_ref, o_ref, *, mode):
    for r in range(x_ref.shape[0] // PROJ_ROWS):
        rows = slice(r * PROJ_ROWS, (r + 1) * PROJ_ROWS)
        acc = jnp.dot(x_ref[rows, :], w_ref[...], preferred_element_type=F32)
        if mode == "gelu":
            acc = jax.nn.gelu(acc)
        elif mode == "gelu_ln":
            acc = _layer_norm(jax.nn.gelu(acc), g_ref[...], b_ref[...])
        o_ref[rows, :] = acc.astype(o_ref.dtype)


def _proj(xb, w_in_b, layer, seg0, nseg, mode, out_dtype, ln_g, ln_b):
    n, d = xb.shape
    tm, tn = PROJ_TM, PROJ_TN
    osz = jnp.dtype(out_dtype).itemsize
    vmem = 2 * (tm * d * 2 + d * tn * 2 + tm * tn * osz) + 4 * PROJ_ROWS * tn * 4
    return pl.pallas_call(
        functools.partial(_proj_kernel, mode=mode),
        grid=(n // tm, nseg),
        in_specs=[
            pl.BlockSpec((tm, d), lambda i, j: (i, 0)),
            pl.BlockSpec((None, d, tn), lambda i, j: (layer, 0, seg0 + j)),
            pl.BlockSpec((None, 1, tn), lambda i, j: (layer, 0, 0)),
            pl.BlockSpec((None, 1, tn), lambda i, j: (layer, 0, 0)),
        ],
        out_specs=pl.BlockSpec((tm, tn), lambda i, j: (i, j)),
        out_shape=jax.ShapeDtypeStruct((n, nseg * tn), out_dtype),
        compiler_params=_params(("parallel", "arbitrary"), vmem),
        name="proj%d_%s" % (seg0, mode),
    )(xb, w_in_b, ln_g, ln_b)


def _pool_kernel(ap_ref, ac_ref, an_ref, wp_ref, sp_ref, o_ref, buf_ref, *, seq, tm):
    i = pl.program_id(0)
    s0 = (i * tm) % seq
    h = POOL_HALO
    buf_ref[0:h, :] = jnp.where(s0 == 0, 0.0, ap_ref[...])
    buf_ref[h:h + tm, :] = ac_ref[...]
    buf_ref[h + tm:2 * h + tm, :] = jnp.where(s0 + tm == seq, 0.0, an_ref[...])
    pos = s0 + lax.broadcasted_iota(jnp.int32, (tm, 1), 0)
    gc = ac_ref.shape[1] // len(POOL_WINDOWS)
    for gi, w in enumerate(POOL_WINDOWS):
        cols = slice(gi * gc, (gi + 1) * gc)
        tot = buf_ref[h - w // 2:h - w // 2 + tm, cols]
        for dlt in range(-w // 2 + 1, w // 2):
            tot = tot + buf_ref[h + dlt:h + dlt + tm, cols]
        cnt = jnp.minimum(pos + w // 2, seq) - jnp.maximum(pos - w // 2, 0)
        p = tot * (1.0 / cnt.astype(F32)) - ac_ref[:, cols]
        y = jnp.dot(p.astype(BF16), wp_ref[gi], preferred_element_type=F32)
        o_ref[:, cols] = (y * sp_ref[:, cols]).astype(BF16)


def _pool(a, w_pool_b, s_pool, layer, seq):
    n, dp = a.shape
    tm, h = POOL_TM, POOL_HALO
    assert seq % tm == 0 and tm % h == 0
    nb = tm // h
    last = n // h - 1
    vmem = 2 * (tm * dp * 4 + 2 * h * dp * 4 + tm * dp * 2) + 4 * dp * dp + (tm + 2 * h) * dp * 4
    return pl.pallas_call(
        functools.partial(_pool_kernel, seq=seq, tm=tm),
        grid=(n // tm,),
        in_specs=[
            pl.BlockSpec((h, dp), lambda i: (jnp.maximum(i * nb - 1, 0), 0)),
            pl.BlockSpec((tm, dp), lambda i: (i, 0)),
            pl.BlockSpec((h, dp), lambda i: (jnp.minimum((i + 1) * nb, last), 0)),
            pl.BlockSpec((None,) + w_pool_b.shape[1:], lambda i: (layer, 0, 0, 0)),
            pl.BlockSpec((None, 1, dp), lambda i: (layer, 0, 0)),
        ],
        out_specs=pl.BlockSpec((tm, dp), lambda i: (i, 0)),
        out_shape=jax.ShapeDtypeStruct((n, dp), BF16),
        scratch_shapes=[pltpu.VMEM((tm + 2 * h, dp), F32)],
        compiler_params=_params(("parallel",), vmem),
        name="pool",
    )(a, a, a, w_pool_b, s_pool)


def _sgu_kernel(u_ref, v_ref, ws_ref, bs_ref, o_ref):
    tm, ds = u_ref.shape
    gc = ds // SGU_GROUPS
    for c in range(tm // CHUNK):
        rows = slice(c * CHUNK, (c + 1) * CHUNK)
        for g in range(SGU_GROUPS):
            cols = slice(g * gc, (g + 1) * gc)
            sg = jnp.dot(ws_ref[g], v_ref[rows, cols], preferred_element_type=F32) + bs_ref[g]
            o_ref[rows, cols] = (u_ref[rows, cols].astype(F32) * sg).astype(BF16)


def _sgu(ug, vn, w_s_b, b_s_col, layer):
    n, ds = ug.shape
    tm = SGU_TM
    vmem = 2 * (3 * tm * ds * 2) + 2 * SGU_GROUPS * CHUNK * (CHUNK * 2 + LANES * 4)
    row = pl.BlockSpec((tm, ds), lambda i: (i, 0))
    return pl.pallas_call(
        _sgu_kernel,
        grid=(n // tm,),
        in_specs=[row, row,
                  pl.BlockSpec((None,) + w_s_b.shape[1:], lambda i: (layer, 0, 0, 0)),
                  pl.BlockSpec((None,) + b_s_col.shape[1:], lambda i: (layer, 0, 0, 0))],
        out_specs=row,
        out_shape=jax.ShapeDtypeStruct((n, ds), BF16),
        compiler_params=_params(("parallel",), vmem),
        name="sgu",
    )(ug, vn, w_s_b, b_s_col)


def _na_edge_tiles():
    return -(-(NA_KH // 2) // NA_Q_ROWS)


def _na_bias_table(rpb_l):
    n_edge = _na_edge_tiles()
    rows = 2 * NA_KH + NA_Q_ROWS * (2 * n_edge + 1)
    n_tiles = rows // NA_Q_ROWS
    tiles = list(range(n_edge)) + [n_edge] + list(range(n_tiles - n_edge, n_tiles))
    c = np.arange(GRID_W)
    c0 = np.clip(c - NA_KW // 2, 0, GRID_W - NA_KW)
    col_ok = (c[None, :] >= c0[:, None]) & (c[None, :] < c0[:, None] + NA_KW)
    dc = c[None, :] - c[:, None] + (NA_KW - 1)
    onehot = ((dc[None] == np.arange(2 * NA_KW - 1)[:, None, None]) & col_ok[None]).astype(np.float32)
    toep = jnp.einsum("hrd,dck->hcrk", rpb_l, onehot, precision=lax.Precision.HIGHEST)
    toep = jnp.where(col_ok[None, :, None, :], toep, NEG_BIG)
    pad = NA_BAND
    toep = jnp.pad(toep, ((0, 0), (0, 0), (pad, pad), (0, 0)), constant_values=NEG_BIG)
    cases = []
    for rt in tiles:
        rs = rt * NA_Q_ROWS
        b0 = int(np.clip(rs - NA_KH // 2, 0, rows - NA_BAND))
        krow = b0 + np.arange(NA_BAND)
        per_row = []
        for i in range(NA_Q_ROWS):
            qrow = rs + i
            r0 = int(np.clip(qrow - NA_KH // 2, 0, rows - NA_KH))
            ok = (krow >= r0) & (krow < r0 + NA_KH)
            start = b0 - qrow + (NA_KH - 1) + pad
            blk = toep[:, :, start:start + NA_BAND, :]
            per_row.append(jnp.where(ok[None, None, :, None], blk, NEG_BIG))
        cases.append(jnp.stack(per_row, axis=1))
    tab = jnp.stack(cases, axis=0)
    return tab.reshape(len(tiles), NA_HEADS // 2, 2 * NA_Q_ROWS * GRID_W, NA_BAND * GRID_W)


def _na_kernel(q_ref, k_ref, v_ref, bias_ref, o_ref, *, rows):
    tq, tk = NA_Q_ROWS * GRID_W, NA_BAND * GRID_W
    n_sub = NA_STEP_ROWS // NA_Q_ROWS
    n_edge = _na_edge_tiles()
    n_tiles = rows // NA_Q_ROWS
    t0 = pl.program_id(2) * n_sub
    lane = lax.broadcasted_iota(jnp.int32, (1, LANES), 1)
    first_head = lane < NA_HEAD_DIM

    for t in range(n_sub):
        rt = t0 + t
        b0 = jnp.clip(rt * NA_Q_ROWS - NA_KH // 2, 0, rows - NA_BAND)
        koff = pl.multiple_of(b0 * GRID_W, GRID_W)
        kb = k_ref[pl.ds(koff, tk), :]
        vb = v_ref[pl.ds(koff, tk), :]
        q = q_ref[t * tq:(t + 1) * tq, :]
        case = jnp.where(rt < n_edge, rt,
                         jnp.where(rt >= n_tiles - n_edge, rt - (n_tiles - n_edge) + n_edge + 1, n_edge))
        zero = jnp.zeros_like(q)
        q2 = jnp.concatenate([jnp.where(first_head, q, zero), jnp.where(first_head, zero, q)], axis=0)
        q2 = q2 * (NA_HEAD_DIM ** -0.5)
        s = lax.dot_general(q2, kb, (((1,), (1,)), ((), ())), preferred_element_type=F32)
        s = s + bias_ref[case]
        m = jnp.max(s, axis=-1, keepdims=True)
        e = jnp.exp(s - m)
        denom = jnp.sum(e, axis=-1, keepdims=True)
        o2 = jnp.dot(e.astype(BF16), vb, preferred_element_type=F32) * (1.0 / denom)
        o_ref[t * tq:(t + 1) * tq, :] = jnp.where(first_head, o2[:tq], o2[tq:]).astype(BF16)


def _natten(qkv, bias_tab, batch, seq):
    n = qkv.shape[0]
    rows = seq // GRID_W
    assert rows % NA_STEP_ROWS == 0 and rows >= 2 * NA_KH + NA_STEP_ROWS
    d_na = NA_HEADS * NA_HEAD_DIM
    n_pairs = d_na // LANES
    steps = rows // NA_STEP_ROWS
    tqs = NA_STEP_ROWS * GRID_W
    tq, tk = NA_Q_ROWS * GRID_W, NA_BAND * GRID_W
    n_case = bias_tab.shape[0]
    vmem = 2 * (2 * tqs * LANES * 2 + 2 * seq * LANES * 2 + n_case * 2 * tq * tk * 4) + 24 * tq * tk * 4
    return pl.pallas_call(
        functools.partial(_na_kernel, rows=rows),
        grid=(batch, n_pairs, steps),
        in_specs=[
            pl.BlockSpec((tqs, LANES), lambda b, hp, r: (b * steps + r, hp)),
            pl.BlockSpec((seq, LANES), lambda b, hp, r: (b, n_pairs + hp)),
            pl.BlockSpec((seq, LANES), lambda b, hp, r: (b, 2 * n_pairs + hp)),
            pl.BlockSpec((n_case, None, 2 * tq, tk), lambda b, hp, r: (0, hp, 0, 0)),
        ],
        out_specs=pl.BlockSpec((tqs, LANES), lambda b, hp, r: (b * steps + r, hp)),
        out_shape=jax.ShapeDtypeStruct((n, d_na), BF16),
        compiler_params=_params(("parallel", "parallel", "arbitrary"), vmem),
        name="natten",
    )(qkv, qkv, qkv, bias_tab)


def _merge_kernel(x_ref, pa_ref, pb_ref, pc_ref, wg0_ref, wg1_ref, wg2_ref, wa_ref, wb_ref, wc_ref, o_ref):
    x = x_ref[...]
    acc = None
    for br_ref, wg_ref, wbr_ref in ((pa_ref, wg0_ref, wa_ref), (pb_ref, wg1_ref, wb_ref), (pc_ref, wg2_ref, wc_ref)):
        gate = jax.nn.sigmoid(jnp.dot(x, wg_ref[...], preferred_element_type=F32))
        y = jnp.dot(br_ref[...], wbr_ref[...], preferred_element_type=F32)
        acc = gate * y if acc is None else acc + gate * y
    o_ref[...] = acc.astype(BF16)


def _merge(xb, pa, pb, pc, w_in_b, w_a, w_b, w_c, layer):
    n, d = xb.shape
    dbr = pa.shape[1]
    tm, tn = MERGE_TM, MERGE_TN
    gate0 = (w_in_b.shape[2] - N_BRANCH * d) // tn
    per = d // tn
    vmem = 2 * (tm * d * 2 + 3 * tm * dbr * 2 + 3 * d * tn * 2 + 3 * dbr * tn * 2 + tm * tn * 2) + 4 * tm * tn * 4
    gate_spec = lambda b: pl.BlockSpec((None, d, tn), lambda i, j: (layer, 0, gate0 + b * per + j))
    br_spec = pl.BlockSpec((tm, dbr), lambda i, j: (i, 0))
    wbr_spec = pl.BlockSpec((None, dbr, tn), lambda i, j: (layer, 0, j))
    return pl.pallas_call(
        _merge_kernel,
        grid=(n // tm, per),
        in_specs=[pl.BlockSpec((tm, d), lambda i, j: (i, 0)), br_spec, br_spec, br_spec,
                  gate_spec(0), gate_spec(1), gate_spec(2), wbr_spec, wbr_spec, wbr_spec],
        out_specs=pl.BlockSpec((tm, tn), lambda i, j: (i, j)),
        out_shape=jax.ShapeDtypeStruct((n, d), BF16),
        compiler_params=_params(("parallel", "arbitrary"), vmem),
        name="merge",
    )(xb, pa, pb, pc, w_in_b, w_in_b, w_in_b, w_a, w_b, w_c)


def _out_ln_kernel(m_ref, x_ref, w_ref, g_ref, b_ref, of_ref, ob_ref, *, alpha):
    for r in range(m_ref.shape[0] // OUT_ROWS):
        rows = slice(r * OUT_ROWS, (r + 1) * OUT_ROWS)
        y = alpha * x_ref[rows, :] + jnp.dot(m_ref[rows, :], w_ref[...], preferred_element_type=F32)
        y = _layer_norm(y, g_ref[...], b_ref[...])
        of_ref[rows, :] = y
        ob_ref[rows, :] = y.astype(BF16)


def _out_ln(merged, xf, w_out_b, g, b, layer, alpha):
    n, d = xf.shape
    tm = OUT_TM
    vmem = 2 * (tm * d * 2 + tm * d * 4 + d * d * 2 + tm * d * 4 + tm * d * 2) + 4 * OUT_ROWS * d * 4
    row = pl.BlockSpec((tm, d), lambda i: (i, 0))
    vec = pl.BlockSpec((None, 1, d), lambda i: (layer, 0, 0))
    return pl.pallas_call(
        functools.partial(_out_ln_kernel, alpha=alpha),
        grid=(n // tm,),
        in_specs=[row, row, pl.BlockSpec((None, d, d), lambda i: (layer, 0, 0)), vec, vec],
        out_specs=[row, row],
        out_shape=[jax.ShapeDtypeStruct((n, d), F32), jax.ShapeDtypeStruct((n, d), BF16)],
        compiler_params=_params(("parallel",), vmem),
        name="out_ln",
    )(merged, xf, w_out_b, g, b)


def _ffn_kernel(xb_ref, xf_ref, wu_ref, wd_ref, g_ref, b_ref, of_ref, ob_ref, *, alpha):
    f = pl.program_id(1)

    @pl.when(f == 0)
    def _():
        of_ref[...] = alpha * xf_ref[...]

    h = jnp.maximum(jnp.dot(xb_ref[...], wu_ref[...], preferred_element_type=F32), 0.0)
    of_ref[...] += jnp.dot((h * h).astype(BF16), wd_ref[...], preferred_element_type=F32)

    @pl.when(f == pl.num_programs(1) - 1)
    def _():
        y = _layer_norm(of_ref[...], g_ref[...], b_ref[...])
        of_ref[...] = y
        ob_ref[...] = y.astype(BF16)


def _ffn(xb, xf, w_up_b, w_down_b, g, b, layer, alpha):
    n, d = xf.shape
    dff = w_up_b.shape[2]
    tm, tf = FFN_TM, FFN_TF
    vmem = 2 * (tm * d * 2 + tm * d * 4 + d * tf * 2 + tf * d * 2 + tm * d * 4 + tm * d * 2) + 3 * tm * tf * 4
    row = pl.BlockSpec((tm, d), lambda i, f: (i, 0))
    vec = pl.BlockSpec((None, 1, d), lambda i, f: (layer, 0, 0))
    return pl.pallas_call(
        functools.partial(_ffn_kernel, alpha=alpha),
        grid=(n // tm, dff // tf),
        in_specs=[row, row, pl.BlockSpec((None, d, tf), lambda i, f: (layer, 0, f)),
                  pl.BlockSpec((None, tf, d), lambda i, f: (layer, f, 0)), vec, vec],
        out_specs=[row, row],
        out_shape=[jax.ShapeDtypeStruct((n, d), F32), jax.ShapeDtypeStruct((n, d), BF16)],
        compiler_params=_params(("parallel", "arbitrary"), vmem),
        name="ffn",
    )(xb, xf, w_up_b, w_down_b, g, b)


def kernel(x_prompt, x_sample, w_in, w_pool, s_pool, sgu_ln_g, sgu_ln_b, w_s, b_s, rpb, w_br_pool, w_br_sgu,
           w_br_na, w_out, ln1_g, ln1_b, w_up, w_down, ln2_g, ln2_b):
    depth = w_in.shape[0]
    alpha = float((2 * depth) ** 0.25)
    w_in_b, w_pool_b, w_s_b = w_in.astype(BF16), w_pool.astype(BF16), w_s.astype(BF16)
    w_a, w_b, w_c = w_br_pool.astype(BF16), w_br_sgu.astype(BF16), w_br_na.astype(BF16)
    w_out_b, w_up_b, w_down_b = w_out.astype(BF16), w_up.astype(BF16), w_down.astype(BF16)
    vec = lambda v: v[:, None, :]
    sgu_g, sgu_b, s_pool_v, b_s_col = vec(sgu_ln_g), vec(sgu_ln_b), vec(s_pool), b_s[..., None]
    ln1 = (vec(ln1_g), vec(ln1_b))
    ln2 = (vec(ln2_g), vec(ln2_b))
    bias_tabs = [_na_bias_table(rpb[l]) for l in range(depth)]

    outs = []
    for x in (x_prompt, x_sample):
        batch, seq, d = x.shape
        xf = x.reshape(batch * seq, d)
        xb = xf.astype(BF16)
        for l in range(depth):
            a = _proj(xb, w_in_b, l, 0, 1, "plain", F32, sgu_g, sgu_b)
            ug = _proj(xb, w_in_b, l, 1, 1, "gelu", BF16, sgu_g, sgu_b)
            vn = _proj(xb, w_in_b, l, 2, 1, "gelu_ln", BF16, sgu_g, sgu_b)
            qkv = _proj(xb, w_in_b, l, 3, 3, "plain", BF16, sgu_g, sgu_b)
            pa = _pool(a, w_pool_b, s_pool_v, l, seq)
            pb = _sgu(ug, vn, w_s_b, b_s_col, l)
            pc = _natten(qkv, bias_tabs[l], batch, seq)
            merged = _merge(xb, pa, pb, pc, w_in_b, w_a, w_b, w_c, l)
            xf, xb = _out_ln(merged, xf, w_out_b, *ln1, l, alpha)
            xf, xb = _ffn(xb, xf, w_up_b, w_down_b, *ln2, l, alpha)
        outs.append(xf.reshape(batch, seq, d))
    return tuple(outs)
```

```python
import functools

import jax
import jax.numpy as jnp
import numpy as np
from jax import lax
from jax.experimental import pallas as pl
from jax.experimental.pallas import tpu as pltpu

F32 = jnp.float32
BF16 = jnp.bfloat16

GRID_W = 64
POOL_WINDOWS = (2, 4, 8, 16)
POOL_HALO = max(POOL_WINDOWS) // 2
CHUNK = 128
SGU_GROUPS = 8
NA_HEADS = 16
NA_HEAD_DIM = 64
NA_KH = 8
NA_KW = 16
N_BRANCH = 3
LN_EPS = 1e-5
NEG_BIG = -1e30

LANES = 128
V7X_VMEM_BYTES = 64 * 1024 * 1024
VMEM_COMPILER_RESERVE = 6 * 1024 * 1024
VMEM_HEADROOM = 3 * 1024 * 1024

PROJ_TM = 2048
PROJ_TN = 1024
PROJ_ROWS = 256
POOL_TM = 512
SGU_TM = 512
NA_Q_ROWS = 4
NA_BAND = NA_Q_ROWS + NA_KH - 1
NA_STEP_ROWS = 32
MERGE_TM = 1024
MERGE_TN = 512
OUT_TM = 512
OUT_ROWS = 128
FFN_TM = 1024
FFN_TF = 1024
FFN_ROWS = 512
CAST_TM = 1024


def _params(semantics, vmem_estimate):
    limit = min(int(vmem_estimate) + VMEM_COMPILER_RESERVE, V7X_VMEM_BYTES - VMEM_HEADROOM)
    return pltpu.CompilerParams(dimension_semantics=semantics, vmem_limit_bytes=limit)


def _layer_norm(y, g, b):
    mu = jnp.mean(y, axis=-1, keepdims=True)
    d = y - mu
    var = jnp.mean(d * d, axis=-1, keepdims=True)
    return d * lax.rsqrt(var + LN_EPS) * g + b


def _proj_kernel(x_ref, w_ref, g_ref, b_ref, o_ref, *, mode):
    for r in range(x_ref.shape[0] // PROJ_ROWS):
        rows = slice(r * PROJ_ROWS, (r + 1) * PROJ_ROWS)
        acc = jnp.dot(x_ref[rows, :], w_ref[...], preferred_element_type=F32)
        if mode == "gelu":
            acc = jax.nn.gelu(acc)
        elif mode == "gelu_ln":
            acc = _layer_norm(jax.nn.gelu(acc), g_ref[...], b_ref[...])
        o_ref[rows, :] = acc.astype(o_ref.dtype)


def _proj(xb, w_in_b, layer, seg0, nseg, mode, out_dtype, ln_g, ln_b):
    n, d = xb.shape
    tm, tn = PROJ_TM, PROJ_TN
    osz = jnp.dtype(out_dtype).itemsize
    vmem = 2 * (tm * d * 2 + d * tn * 2 + tm * tn * osz) + 4 * PROJ_ROWS * tn * 4
    return pl.pallas_call(
        functools.partial(_proj_kernel, mode=mode),
        grid=(n // tm, nseg),
        in_specs=[
            pl.BlockSpec((tm, d), lambda i, j: (i, 0)),
            pl.BlockSpec((None, d, tn), lambda i, j: (layer, 0, seg0 + j)),
            pl.BlockSpec((None, 1, tn), lambda i, j: (layer, 0, 0)),
            pl.BlockSpec((None, 1, tn), lambda i, j: (layer, 0, 0)),
        ],
        out_specs=pl.BlockSpec((tm, tn), lambda i, j: (i, j)),
        out_shape=jax.ShapeDtypeStruct((n, nseg * tn), out_dtype),
        compiler_params=_params(("parallel", "arbitrary"), vmem),
        name="proj%d_%s" % (seg0, mode),
    )(xb, w_in_b, ln_g, ln_b)


def _proj_cast_kernel(x_ref, w_ref, a_ref, xb_ref):
    for r in range(x_ref.shape[0] // PROJ_ROWS):
        rows = slice(r * PROJ_ROWS, (r + 1) * PROJ_ROWS)
        xb = x_ref[rows, :].astype(BF16)
        xb_ref[rows, :] = xb
        a_ref[rows, :] = jnp.dot(xb, w_ref[...], preferred_element_type=F32)


def _proj_cast(xf, w_in_b, layer):
    n, d = xf.shape
    tm, tn = CAST_TM, PROJ_TN
    vmem = 2 * (tm * d * 4 + d * tn * 2 + tm * tn * 4 + tm * d * 2) + 4 * PROJ_ROWS * tn * 4
    return pl.pallas_call(
        _proj_cast_kernel,
        grid=(n // tm,),
        in_specs=[pl.BlockSpec((tm, d), lambda i: (i, 0)),
                  pl.BlockSpec((None, d, tn), lambda i: (layer, 0, 0))],
        out_specs=[pl.BlockSpec((tm, tn), lambda i: (i, 0)), pl.BlockSpec((tm, d), lambda i: (i, 0))],
        out_shape=[jax.ShapeDtypeStruct((n, tn), F32), jax.ShapeDtypeStruct((n, d), BF16)],
        compiler_params=_params(("parallel",), vmem),
        name="proj0_cast",
    )(xf, w_in_b)


def _pool_kernel(ap_ref, ac_ref, an_ref, wp_ref, sp_ref, o_ref, buf_ref, *, seq, tm):
    i = pl.program_id(0)
    s0 = (i * tm) % seq
    h = POOL_HALO
    buf_ref[0:h, :] = jnp.where(s0 == 0, 0.0, ap_ref[...])
    buf_ref[h:h + tm, :] = ac_ref[...]
    buf_ref[h + tm:2 * h + tm, :] = jnp.where(s0 + tm == seq, 0.0, an_ref[...])
    pos = s0 + lax.broadcasted_iota(jnp.int32, (tm, 1), 0)
    gc = ac_ref.shape[1] // len(POOL_WINDOWS)
    for gi, w in enumerate(POOL_WINDOWS):
        cols = slice(gi * gc, (gi + 1) * gc)
        tot = buf_ref[h - w // 2:h - w // 2 + tm, cols]
        for dlt in range(-w // 2 + 1, w // 2):
            tot = tot + buf_ref[h + dlt:h + dlt + tm, cols]
        cnt = jnp.minimum(pos + w // 2, seq) - jnp.maximum(pos - w // 2, 0)
        p = tot * (1.0 / cnt.astype(F32)) - ac_ref[:, cols]
        y = jnp.dot(p.astype(BF16), wp_ref[gi], preferred_element_type=F32)
        o_ref[:, cols] = (y * sp_ref[:, cols]).astype(BF16)


def _pool(a, w_pool_b, s_pool, layer, seq):
    n, dp = a.shape
    tm, h = POOL_TM, POOL_HALO
    assert seq % tm == 0 and tm % h == 0
    nb = tm // h
    last = n // h - 1
    vmem = 2 * (tm * dp * 4 + 2 * h * dp * 4 + tm * dp * 2) + 4 * dp * dp + (tm + 2 * h) * dp * 4
    return pl.pallas_call(
        functools.partial(_pool_kernel, seq=seq, tm=tm),
        grid=(n // tm,),
        in_specs=[
            pl.BlockSpec((h, dp), lambda i: (jnp.maximum(i * nb - 1, 0), 0)),
            pl.BlockSpec((tm, dp), lambda i: (i, 0)),
            pl.BlockSpec((h, dp), lambda i: (jnp.minimum((i + 1) * nb, last), 0)),
            pl.BlockSpec((None,) + w_pool_b.shape[1:], lambda i: (layer, 0, 0, 0)),
            pl.BlockSpec((None, 1, dp), lambda i: (layer, 0, 0)),
        ],
        out_specs=pl.BlockSpec((tm, dp), lambda i: (i, 0)),
        out_shape=jax.ShapeDtypeStruct((n, dp), BF16),
        scratch_shapes=[pltpu.VMEM((tm + 2 * h, dp), F32)],
        compiler_params=_params(("parallel",), vmem),
        name="pool",
    )(a, a, a, w_pool_b, s_pool)


def _sgu_kernel(u_ref, v_ref, ws_ref, bs_ref, o_ref):
    tm, ds = u_ref.shape
    gc = ds // SGU_GROUPS
    for c in range(tm // CHUNK):
        rows = slice(c * CHUNK, (c + 1) * CHUNK)
        for g in range(SGU_GROUPS):
            cols = slice(g * gc, (g + 1) * gc)
            sg = jnp.dot(ws_ref[g], v_ref[rows, cols], preferred_element_type=F32) + bs_ref[g]
            o_ref[rows, cols] = (u_ref[rows, cols].astype(F32) * sg).astype(BF16)


def _sgu(ug, vn, w_s_b, b_s_col, layer):
    n, ds = ug.shape
    tm = SGU_TM
    vmem = 2 * (3 * tm * ds * 2) + 2 * SGU_GROUPS * CHUNK * (CHUNK * 2 + LANES * 4)
    row = pl.BlockSpec((tm, ds), lambda i: (i, 0))
    return pl.pallas_call(
        _sgu_kernel,
        grid=(n // tm,),
        in_specs=[row, row,
                  pl.BlockSpec((None,) + w_s_b.shape[1:], lambda i: (layer, 0, 0, 0)),
                  pl.BlockSpec((None,) + b_s_col.shape[1:], lambda i: (layer, 0, 0, 0))],
        out_specs=row,
        out_shape=jax.ShapeDtypeStruct((n, ds), BF16),
        compiler_params=_params(("parallel",), vmem),
        name="sgu",
    )(ug, vn, w_s_b, b_s_col)


def _na_edge_tiles():
    return -(-(NA_KH // 2) // NA_Q_ROWS)


def _na_bias_table(rpb_l):
    n_edge = _na_edge_tiles()
    rows = 2 * NA_KH + NA_Q_ROWS * (2 * n_edge + 1)
    n_tiles = rows // NA_Q_ROWS
    tiles = list(range(n_edge)) + [n_edge] + list(range(n_tiles - n_edge, n_tiles))
    c = np.arange(GRID_W)
    c0 = np.clip(c - NA_KW // 2, 0, GRID_W - NA_KW)
    col_ok = (c[None, :] >= c0[:, None]) & (c[None, :] < c0[:, None] + NA_KW)
    dc = c[None, :] - c[:, None] + (NA_KW - 1)
    onehot = ((dc[None] == np.arange(2 * NA_KW - 1)[:, None, None]) & col_ok[None]).astype(np.float32)
    toep = jnp.einsum("hrd,dck->hcrk", rpb_l, onehot, precision=lax.Precision.HIGHEST)
    toep = jnp.where(col_ok[None, :, None, :], toep, NEG_BIG)
    pad = NA_BAND
    toep = jnp.pad(toep, ((0, 0), (0, 0), (pad, pad), (0, 0)), constant_values=NEG_BIG)
    cases = []
    for rt in tiles:
        rs = rt * NA_Q_ROWS
        b0 = int(np.clip(rs - NA_KH // 2, 0, rows - NA_BAND))
        krow = b0 + np.arange(NA_BAND)
        per_row = []
        for i in range(NA_Q_ROWS):
            qrow = rs + i
            r0 = int(np.clip(qrow - NA_KH // 2, 0, rows - NA_KH))
            ok = (krow >= r0) & (krow < r0 + NA_KH)
            start = b0 - qrow + (NA_KH - 1) + pad
            blk = toep[:, :, start:start + NA_BAND, :]
            per_row.append(jnp.where(ok[None, None, :, None], blk, NEG_BIG))
        cases.append(jnp.stack(per_row, axis=1))
    tab = jnp.stack(cases, axis=0)
    return tab.reshape(len(tiles), NA_HEADS // 2, 2 * NA_Q_ROWS * GRID_W, NA_BAND * GRID_W)


def _na_kernel(q_ref, k_ref, v_ref, bias_ref, o_ref, *, rows):
    tq, tk = NA_Q_ROWS * GRID_W, NA_BAND * GRID_W
    n_sub = NA_STEP_ROWS // NA_Q_ROWS
    n_edge = _na_edge_tiles()
    n_tiles = rows // NA_Q_ROWS
    t0 = pl.program_id(2) * n_sub
    lane = lax.broadcasted_iota(jnp.int32, (1, LANES), 1)
    first_head = lane < NA_HEAD_DIM

    for t in range(n_sub):
        rt = t0 + t
        b0 = jnp.clip(rt * NA_Q_ROWS - NA_KH // 2, 0, rows - NA_BAND)
        koff = pl.multiple_of(b0 * GRID_W, GRID_W)
        kb = k_ref[pl.ds(koff, tk), :]
        vb = v_ref[pl.ds(koff, tk), :]
        q = q_ref[t * tq:(t + 1) * tq, :]
        case = jnp.where(rt < n_edge, rt,
                         jnp.where(rt >= n_tiles - n_edge, rt - (n_tiles - n_edge) + n_edge + 1, n_edge))
        zero = jnp.zeros_like(q)
        q2 = jnp.concatenate([jnp.where(first_head, q, zero), jnp.where(first_head, zero, q)], axis=0)
        q2 = q2 * (NA_HEAD_DIM ** -0.5)
        s = lax.dot_general(q2, kb, (((1,), (1,)), ((), ())), preferred_element_type=F32)
        s = s + bias_ref[case]
        m = jnp.max(s, axis=-1, keepdims=True)
        e = jnp.exp(s - m)
        denom = jnp.sum(e, axis=-1, keepdims=True)
        o2 = jnp.dot(e.astype(BF16), vb, preferred_element_type=F32) * (1.0 / denom)
        o_ref[t * tq:(t + 1) * tq, :] = jnp.where(first_head, o2[:tq], o2[tq:]).astype(BF16)


def _natten(qkv, bias_tab, batch, seq):
    n = qkv.shape[0]
    rows = seq // GRID_W
    assert rows % NA_STEP_ROWS == 0 and rows >= 2 * NA_KH + NA_STEP_ROWS
    d_na = NA_HEADS * NA_HEAD_DIM
    n_pairs = d_na // LANES
    steps = rows // NA_STEP_ROWS
    tqs = NA_STEP_ROWS * GRID_W
    tq, tk = NA_Q_ROWS * GRID_W, NA_BAND * GRID_W
    n_case = bias_tab.shape[0]
    vmem = 2 * (2 * tqs * LANES * 2 + 2 * seq * LANES * 2 + n_case * 2 * tq * tk * 4) + 24 * tq * tk * 4
    return pl.pallas_call(
        functools.partial(_na_kernel, rows=rows),
        grid=(batch, n_pairs, steps),
        in_specs=[
            pl.BlockSpec((tqs, LANES), lambda b, hp, r: (b * steps + r, hp)),
            pl.BlockSpec((seq, LANES), lambda b, hp, r: (b, n_pairs + hp)),
            pl.BlockSpec((seq, LANES), lambda b, hp, r: (b, 2 * n_pairs + hp)),
            pl.BlockSpec((n_case, None, 2 * tq, tk), lambda b, hp, r: (0, hp, 0, 0)),
        ],
        out_specs=pl.BlockSpec((tqs, LANES), lambda b, hp, r: (b * steps + r, hp)),
        out_shape=jax.ShapeDtypeStruct((n, d_na), BF16),
        compiler_params=_params(("parallel", "parallel", "arbitrary"), vmem),
        name="natten",
    )(qkv, qkv, qkv, bias_tab)


def _merge_kernel(x_ref, pa_ref, pb_ref, pc_ref, wg0_ref, wg1_ref, wg2_ref, wa_ref, wb_ref, wc_ref, o_ref):
    x = x_ref[...]
    acc = None
    for br_ref, wg_ref, wbr_ref in ((pa_ref, wg0_ref, wa_ref), (pb_ref, wg1_ref, wb_ref), (pc_ref, wg2_ref, wc_ref)):
        gate = jax.nn.sigmoid(jnp.dot(x, wg_ref[...], preferred_element_type=F32))
        y = jnp.dot(br_ref[...], wbr_ref[...], preferred_element_type=F32)
        acc = gate * y if acc is None else acc + gate * y
    o_ref[...] = acc.astype(BF16)


def _merge(xb, pa, pb, pc, w_in_b, w_a, w_b, w_c, layer):
    n, d = xb.shape
    dbr = pa.shape[1]
    tm, tn = MERGE_TM, MERGE_TN
    gate0 = (w_in_b.shape[2] - N_BRANCH * d) // tn
    per = d // tn
    vmem = 2 * (tm * d * 2 + 3 * tm * dbr * 2 + 3 * d * tn * 2 + 3 * dbr * tn * 2 + tm * tn * 2) + 4 * tm * tn * 4
    gate_spec = lambda b: pl.BlockSpec((None, d, tn), lambda i, j: (layer, 0, gate0 + b * per + j))
    br_spec = pl.BlockSpec((tm, dbr), lambda i, j: (i, 0))
    wbr_spec = pl.BlockSpec((None, dbr, tn), lambda i, j: (layer, 0, j))
    return pl.pallas_call(
        _merge_kernel,
        grid=(n // tm, per),
        in_specs=[pl.BlockSpec((tm, d), lambda i, j: (i, 0)), br_spec, br_spec, br_spec,
                  gate_spec(0), gate_spec(1), gate_spec(2), wbr_spec, wbr_spec, wbr_spec],
        out_specs=pl.BlockSpec((tm, tn), lambda i, j: (i, j)),
        out_shape=jax.ShapeDtypeStruct((n, d), BF16),
        compiler_params=_params(("parallel", "arbitrary"), vmem),
        name="merge",
    )(xb, pa, pb, pc, w_in_b, w_in_b, w_in_b, w_a, w_b, w_c)


def _out_ln_kernel(m_ref, x_ref, w_ref, g_ref, b_ref, of_ref, ob_ref, *, alpha):
    for r in range(m_ref.shape[0] // OUT_ROWS):
        rows = slice(r * OUT_ROWS, (r + 1) * OUT_ROWS)
        y = alpha * x_ref[rows, :] + jnp.dot(m_ref[rows, :], w_ref[...], preferred_element_type=F32)
        y = _layer_norm(y, g_ref[...], b_ref[...])
        of_ref[rows, :] = y
        ob_ref[rows, :] = y.astype(BF16)


def _out_ln(merged, xf, w_out_b, g, b, layer, alpha):
    n, d = xf.shape
    tm = OUT_TM
    vmem = 2 * (tm * d * 2 + tm * d * 4 + d * d * 2 + tm * d * 4 + tm * d * 2) + 4 * OUT_ROWS * d * 4
    row = pl.BlockSpec((tm, d), lambda i: (i, 0))
    vec = pl.BlockSpec((None, 1, d), lambda i: (layer, 0, 0))
    return pl.pallas_call(
        functools.partial(_out_ln_kernel, alpha=alpha),
        grid=(n // tm,),
        in_specs=[row, row, pl.BlockSpec((None, d, d), lambda i: (layer, 0, 0)), vec, vec],
        out_specs=[row, row],
        out_shape=[jax.ShapeDtypeStruct((n, d), F32), jax.ShapeDtypeStruct((n, d), BF16)],
        compiler_params=_params(("parallel",), vmem),
        name="out_ln",
    )(merged, xf, w_out_b, g, b)


def _ffn_kernel(xb_ref, xf_hbm, wu_ref, wd_ref, g_ref, b_ref, of_ref, ob_ref, xf_buf, xf_sem, *, alpha):
    i, f = pl.program_id(0), pl.program_id(1)
    tm = xb_ref.shape[0]

    def xf_copy(tile):
        return pltpu.make_async_copy(xf_hbm.at[pl.ds(tile * tm, tm), :], xf_buf, xf_sem)

    @pl.when((i == 0) & (f == 0))
    def _():
        xf_copy(0).start()

    @pl.when(f == 0)
    def _():
        xf_copy(i).wait()
        of_ref[...] = alpha * xf_buf[...]

    @pl.when((f == 1) & (i + 1 < pl.num_programs(0)))
    def _():
        xf_copy(i + 1).start()

    for r in range(tm // FFN_ROWS):
        rows = slice(r * FFN_ROWS, (r + 1) * FFN_ROWS)
        h = jnp.maximum(jnp.dot(xb_ref[rows, :], wu_ref[...], preferred_element_type=F32), 0.0)
        of_ref[rows, :] += jnp.dot((h * h).astype(BF16), wd_ref[...], preferred_element_type=F32)

    @pl.when(f == pl.num_programs(1) - 1)
    def _():
        for r in range(tm // FFN_ROWS):
            rows = slice(r * FFN_ROWS, (r + 1) * FFN_ROWS)
            y = _layer_norm(of_ref[rows, :], g_ref[...], b_ref[...])
            of_ref[rows, :] = y
            ob_ref[rows, :] = y.astype(BF16)


def _ffn(xb, xf, w_up_b, w_down_b, g, b, layer, alpha):
    n, d = xf.shape
    dff = w_up_b.shape[2]
    tm, tf = FFN_TM, FFN_TF
    assert dff // tf >= 2
    vmem = (2 * (tm * d * 2 + d * tf * 2 + tf * d * 2 + tm * d * 4) + tm * d * 4 + tm * d * 2
            + FFN_ROWS * tf * 6 + FFN_ROWS * d * 4)
    row = pl.BlockSpec((tm, d), lambda i, f: (i, 0))
    vec = pl.BlockSpec((None, 1, d), lambda i, f: (layer, 0, 0))
    return pl.pallas_call(
        functools.partial(_ffn_kernel, alpha=alpha),
        grid=(n // tm, dff // tf),
        in_specs=[row, pl.BlockSpec(memory_space=pl.ANY),
                  pl.BlockSpec((None, d, tf), lambda i, f: (layer, 0, f)),
                  pl.BlockSpec((None, tf, d), lambda i, f: (layer, f, 0)), vec, vec],
        out_specs=[row, pl.BlockSpec((tm, d), lambda i, f: (i, 0), pipeline_mode=pl.Buffered(1))],
        out_shape=[jax.ShapeDtypeStruct((n, d), F32), jax.ShapeDtypeStruct((n, d), BF16)],
        scratch_shapes=[pltpu.VMEM((tm, d), F32), pltpu.SemaphoreType.DMA(())],
        compiler_params=_params(("arbitrary", "arbitrary"), vmem),
        name="ffn",
    )(xb, xf, w_up_b, w_down_b, g, b)


def kernel(x_prompt, x_sample, w_in, w_pool, s_pool, sgu_ln_g, sgu_ln_b, w_s, b_s, rpb, w_br_pool, w_br_sgu,
           w_br_na, w_out, ln1_g, ln1_b, w_up, w_down, ln2_g, ln2_b):
    depth = w_in.shape[0]
    alpha = float((2 * depth) ** 0.25)
    w_in_b, w_pool_b, w_s_b = w_in.astype(BF16), w_pool.astype(BF16), w_s.astype(BF16)
    w_a, w_b, w_c = w_br_pool.astype(BF16), w_br_sgu.astype(BF16), w_br_na.astype(BF16)
    w_out_b, w_up_b, w_down_b = w_out.astype(BF16), w_up.astype(BF16), w_down.astype(BF16)
    vec = lambda v: v[:, None, :]
    sgu_g, sgu_b, s_pool_v, b_s_col = vec(sgu_ln_g), vec(sgu_ln_b), vec(s_pool), b_s[..., None]
    ln1 = (vec(ln1_g), vec(ln1_b))
    ln2 = (vec(ln2_g), vec(ln2_b))
    bias_tabs = [_na_bias_table(rpb[l]) for l in range(depth)]

    outs = []
    for x in (x_prompt, x_sample):
        batch, seq, d = x.shape
        xf = x.reshape(batch * seq, d)
        for l in range(depth):
            if l == 0:
                a, xb = _proj_cast(xf, w_in_b, l)
            else:
                a = _proj(xb, w_in_b, l, 0, 1, "plain", F32, sgu_g, sgu_b)
            ug = _proj(xb, w_in_b, l, 1, 1, "gelu", BF16, sgu_g, sgu_b)
            vn = _proj(xb, w_in_b, l, 2, 1, "gelu_ln", BF16, sgu_g, sgu_b)
            qkv = _proj(xb, w_in_b, l, 3, 3, "plain", BF16, sgu_g, sgu_b)
            pa = _pool(a, w_pool_b, s_pool_v, l, seq)
            pb = _sgu(ug, vn, w_s_b, b_s_col, l)
            pc = _natten(qkv, bias_tabs[l], batch, seq)
            merged = _merge(xb, pa, pb, pc, w_in_b, w_a, w_b, w_c, l)
            xf, xb = _out_ln(merged, xf, w_out_b, *ln1, l, alpha)
            xf, xb = _ffn(xb, xf, w_up_b, w_down_b, *ln2, l, alpha)
        outs.append(xf.reshape(batch, seq, d))
    return tuple(outs)
```

```python
import functools

import jax
import jax.numpy as jnp
import numpy as np
from jax import lax
from jax.experimental import pallas as pl
from jax.experimental.pallas import tpu as pltpu

F32 = jnp.float32
BF16 = jnp.bfloat16

GRID_W = 64
POOL_WINDOWS = (2, 4, 8, 16)
POOL_HALO = max(POOL_WINDOWS) // 2
CHUNK = 128
SGU_GROUPS = 8
NA_HEADS = 16
NA_HEAD_DIM = 64
NA_KH = 8
NA_KW = 16
N_BRANCH = 3
LN_EPS = 1e-5
NEG_BIG = -1e30

LANES = 128
V7X_VMEM_BYTES = 64 * 1024 * 1024
VMEM_COMPILER_RESERVE = 6 * 1024 * 1024
VMEM_HEADROOM = 3 * 1024 * 1024

PROJ_TM = 2048
PROJ_TN = 1024
PROJ_ROWS = 256
POOL_TM = 512
SGU_TM = 512
NA_Q_ROWS = 4
NA_BAND = NA_Q_ROWS + NA_KH - 1
NA_STEP_ROWS = 64
MERGE_TM = 1024
MERGE_TN = 512
OUT_TM = 512
OUT_ROWS = 128
FFN_TM = 512
FFN_TF = 1024
FFN_ROWS = 256
CAST_TM = 1024


def _params(semantics, vmem_estimate):
    limit = min(int(vmem_estimate) + VMEM_COMPILER_RESERVE, V7X_VMEM_BYTES - VMEM_HEADROOM)
    return pltpu.CompilerParams(dimension_semantics=semantics, vmem_limit_bytes=limit)


def _layer_norm(y, g, b):
    mu = jnp.mean(y, axis=-1, keepdims=True)
    d = y - mu
    var = jnp.mean(d * d, axis=-1, keepdims=True)
    return d * lax.rsqrt(var + LN_EPS) * g + b


def _proj_kernel(x_ref, w_ref, g_ref, b_ref, o_ref, *, mode):
    for r in range(x_ref.shape[0] // PROJ_ROWS):
        rows = slice(r * PROJ_ROWS, (r + 1) * PROJ_ROWS)
        acc = jnp.dot(x_ref[rows, :], w_ref[...], preferred_element_type=F32)
        if mode == "gelu":
            acc = jax.nn.gelu(acc)
        elif mode == "gelu_ln":
            acc = _layer_norm(jax.nn.gelu(acc), g_ref[...], b_ref[...])
        o_ref[rows, :] = acc.astype(o_ref.dtype)


def _proj(xb, w_in_b, layer, seg0, nseg, mode, out_dtype, ln_g, ln_b):
    n, d = xb.shape
    tm, tn = PROJ_TM, PROJ_TN
    osz = jnp.dtype(out_dtype).itemsize
    vmem = 2 * (tm * d * 2 + d * tn * 2 + tm * tn * osz) + 4 * PROJ_ROWS * tn * 4
    return pl.pallas_call(
        functools.partial(_proj_kernel, mode=mode),
        grid=(n // tm, nseg),
        in_specs=[
            pl.BlockSpec((tm, d), lambda i, j: (i, 0)),
            pl.BlockSpec((None, d, tn), lambda i, j: (layer, 0, seg0 + j)),
            pl.BlockSpec((None, 1, tn), lambda i, j: (layer, 0, 0)),
            pl.BlockSpec((None, 1, tn), lambda i, j: (layer, 0, 0)),
        ],
        out_specs=pl.BlockSpec((tm, tn), lambda i, j: (i, j)),
        out_shape=jax.ShapeDtypeStruct((n, nseg * tn), out_dtype),
        compiler_params=_params(("parallel", "arbitrary"), vmem),
        name="proj%d_%s" % (seg0, mode),
    )(xb, w_in_b, ln_g, ln_b)


def _proj_cast_kernel(x_ref, w_ref, a_ref, xb_ref):
    for r in range(x_ref.shape[0] // PROJ_ROWS):
        rows = slice(r * PROJ_ROWS, (r + 1) * PROJ_ROWS)
        xb = x_ref[rows, :].astype(BF16)
        xb_ref[rows, :] = xb
        a_ref[rows, :] = jnp.dot(xb, w_ref[...], preferred_element_type=F32)


def _proj_cast(xf, w_in_b, layer):
    n, d = xf.shape
    tm, tn = CAST_TM, PROJ_TN
    vmem = 2 * (tm * d * 4 + d * tn * 2 + tm * tn * 4 + tm * d * 2) + 4 * PROJ_ROWS * tn * 4
    return pl.pallas_call(
        _proj_cast_kernel,
        grid=(n // tm,),
        in_specs=[pl.BlockSpec((tm, d), lambda i: (i, 0)),
                  pl.BlockSpec((None, d, tn), lambda i: (layer, 0, 0))],
        out_specs=[pl.BlockSpec((tm, tn), lambda i: (i, 0)), pl.BlockSpec((tm, d), lambda i: (i, 0))],
        out_shape=[jax.ShapeDtypeStruct((n, tn), F32), jax.ShapeDtypeStruct((n, d), BF16)],
        compiler_params=_params(("parallel",), vmem),
        name="proj0_cast",
    )(xf, w_in_b)


def _pool_kernel(ap_ref, ac_ref, an_ref, wp_ref, sp_ref, o_ref, buf_ref, *, seq, tm):
    i = pl.program_id(0)
    s0 = (i * tm) % seq
    h = POOL_HALO
    buf_ref[0:h, :] = jnp.where(s0 == 0, 0.0, ap_ref[...])
    buf_ref[h:h + tm, :] = ac_ref[...]
    buf_ref[h + tm:2 * h + tm, :] = jnp.where(s0 + tm == seq, 0.0, an_ref[...])
    pos = s0 + lax.broadcasted_iota(jnp.int32, (tm, 1), 0)
    gc = ac_ref.shape[1] // len(POOL_WINDOWS)
    for gi, w in enumerate(POOL_WINDOWS):
        cols = slice(gi * gc, (gi + 1) * gc)
        tot = buf_ref[h - w // 2:h - w // 2 + tm, cols]
        for dlt in range(-w // 2 + 1, w // 2):
            tot = tot + buf_ref[h + dlt:h + dlt + tm, cols]
        cnt = jnp.minimum(pos + w // 2, seq) - jnp.maximum(pos - w // 2, 0)
        p = tot * (1.0 / cnt.astype(F32)) - ac_ref[:, cols]
        y = jnp.dot(p.astype(BF16), wp_ref[gi], preferred_element_type=F32)
        o_ref[:, cols] = (y * sp_ref[:, cols]).astype(BF16)


def _pool(a, w_pool_b, s_pool, layer, seq):
    n, dp = a.shape
    tm, h = POOL_TM, POOL_HALO
    assert seq % tm == 0 and tm % h == 0
    nb = tm // h
    last = n // h - 1
    vmem = 2 * (tm * dp * 4 + 2 * h * dp * 4 + tm * dp * 2) + 4 * dp * dp + (tm + 2 * h) * dp * 4
    return pl.pallas_call(
        functools.partial(_pool_kernel, seq=seq, tm=tm),
        grid=(n // tm,),
        in_specs=[
            pl.BlockSpec((h, dp), lambda i: (jnp.maximum(i * nb - 1, 0), 0)),
            pl.BlockSpec((tm, dp), lambda i: (i, 0)),
            pl.BlockSpec((h, dp), lambda i: (jnp.minimum((i + 1) * nb, last), 0)),
            pl.BlockSpec((None,) + w_pool_b.shape[1:], lambda i: (layer, 0, 0, 0)),
            pl.BlockSpec((None, 1, dp), lambda i: (layer, 0, 0)),
        ],
        out_specs=pl.BlockSpec((tm, dp), lambda i: (i, 0)),
        out_shape=jax.ShapeDtypeStruct((n, dp), BF16),
        scratch_shapes=[pltpu.VMEM((tm + 2 * h, dp), F32)],
        compiler_params=_params(("parallel",), vmem),
        name="pool",
    )(a, a, a, w_pool_b, s_pool)


def _sgu_kernel(u_ref, v_ref, ws_ref, bs_ref, o_ref):
    tm, ds = u_ref.shape
    gc = ds // SGU_GROUPS
    n_chunks = tm // CHUNK
    for g in range(SGU_GROUPS):
        cols = slice(g * gc, (g + 1) * gc)
        v_all = jnp.concatenate([v_ref[c * CHUNK:(c + 1) * CHUNK, cols] for c in range(n_chunks)], axis=1)
        sg = jnp.dot(ws_ref[g], v_all, preferred_element_type=F32) + bs_ref[g]
        for c in range(n_chunks):
            rows = slice(c * CHUNK, (c + 1) * CHUNK)
            o_ref[rows, cols] = (u_ref[rows, cols].astype(F32) * sg[:, c * gc:(c + 1) * gc]).astype(BF16)


def _sgu(ug, vn, w_s_b, b_s_col, layer):
    n, ds = ug.shape
    tm = SGU_TM
    vmem = 2 * (3 * tm * ds * 2) + 2 * SGU_GROUPS * CHUNK * (CHUNK * 2 + LANES * 4)
    row = pl.BlockSpec((tm, ds), lambda i: (i, 0))
    return pl.pallas_call(
        _sgu_kernel,
        grid=(n // tm,),
        in_specs=[row, row,
                  pl.BlockSpec((None,) + w_s_b.shape[1:], lambda i: (layer, 0, 0, 0)),
                  pl.BlockSpec((None,) + b_s_col.shape[1:], lambda i: (layer, 0, 0, 0))],
        out_specs=row,
        out_shape=jax.ShapeDtypeStruct((n, ds), BF16),
        compiler_params=_params(("parallel",), vmem),
        name="sgu",
    )(ug, vn, w_s_b, b_s_col)


def _na_edge_tiles():
    return -(-(NA_KH // 2) // NA_Q_ROWS)


def _na_bias_table(rpb_l):
    n_edge = _na_edge_tiles()
    rows = 2 * NA_KH + NA_Q_ROWS * (2 * n_edge + 1)
    n_tiles = rows // NA_Q_ROWS
    tiles = list(range(n_edge)) + [n_edge] + list(range(n_tiles - n_edge, n_tiles))
    c = np.arange(GRID_W)
    c0 = np.clip(c - NA_KW // 2, 0, GRID_W - NA_KW)
    col_ok = (c[None, :] >= c0[:, None]) & (c[None, :] < c0[:, None] + NA_KW)
    dc = c[None, :] - c[:, None] + (NA_KW - 1)
    onehot = ((dc[None] == np.arange(2 * NA_KW - 1)[:, None, None]) & col_ok[None]).astype(np.float32)
    toep = jnp.einsum("hrd,dck->hcrk", rpb_l, onehot, precision=lax.Precision.HIGHEST)
    toep = jnp.where(col_ok[None, :, None, :], toep, NEG_BIG)
    pad = NA_BAND
    toep = jnp.pad(toep, ((0, 0), (0, 0), (pad, pad), (0, 0)), constant_values=NEG_BIG)
    toep = toep.reshape(NA_HEADS, GRID_W, -1)
    cases = []
    for rt in tiles:
        rs = rt * NA_Q_ROWS
        b0 = int(np.clip(rs - NA_KH // 2, 0, rows - NA_BAND))
        krow = b0 + np.arange(NA_BAND)
        per_row = []
        for i in range(NA_Q_ROWS):
            qrow = rs + i
            r0 = int(np.clip(qrow - NA_KH // 2, 0, rows - NA_KH))
            ok = np.repeat((krow >= r0) & (krow < r0 + NA_KH), GRID_W)
            start = (b0 - qrow + (NA_KH - 1) + pad) * GRID_W
            blk = toep[:, :, start:start + NA_BAND * GRID_W]
            per_row.append(jnp.where(ok[None, None, :], blk, NEG_BIG))
        cases.append(jnp.stack(per_row, axis=1))
    tab = jnp.stack(cases, axis=0)
    return tab.reshape(len(tiles), NA_HEADS // 2, 2 * NA_Q_ROWS * GRID_W, NA_BAND * GRID_W)


def _na_kernel(q_ref, k_ref, v_ref, bias_ref, o_ref, *, rows):
    tq, tk = NA_Q_ROWS * GRID_W, NA_BAND * GRID_W
    n_sub = NA_STEP_ROWS // NA_Q_ROWS
    n_edge = _na_edge_tiles()
    n_tiles = rows // NA_Q_ROWS
    t0 = pl.program_id(2) * n_sub
    lane = lax.broadcasted_iota(jnp.int32, (1, LANES), 1)
    first_head = lane < NA_HEAD_DIM

    for t in range(n_sub):
        rt = t0 + t
        b0 = jnp.clip(rt * NA_Q_ROWS - NA_KH // 2, 0, rows - NA_BAND)
        koff = pl.multiple_of(b0 * GRID_W, GRID_W)
        kb = k_ref[pl.ds(koff, tk), :]
        vb = v_ref[pl.ds(koff, tk), :]
        q = q_ref[t * tq:(t + 1) * tq, :]
        case = jnp.where(rt < n_edge, rt,
                         jnp.where(rt >= n_tiles - n_edge, rt - (n_tiles - n_edge) + n_edge + 1, n_edge))
        zero = jnp.zeros_like(q)
        q2 = jnp.concatenate([jnp.where(first_head, q, zero), jnp.where(first_head, zero, q)], axis=0)
        q2 = q2 * (NA_HEAD_DIM ** -0.5)
        s = lax.dot_general(q2, kb, (((1,), (1,)), ((), ())), preferred_element_type=F32)
        s = s + bias_ref[case]
        m = jnp.max(s, axis=-1, keepdims=True)
        e = jnp.exp(s - m)
        denom = jnp.sum(e, axis=-1, keepdims=True)
        o2 = jnp.dot(e.astype(BF16), vb, preferred_element_type=F32) * (1.0 / denom)
        o_ref[t * tq:(t + 1) * tq, :] = jnp.where(first_head, o2[:tq], o2[tq:]).astype(BF16)


def _natten(qkv, bias_tab, batch, seq):
    n = qkv.shape[0]
    rows = seq // GRID_W
    assert rows % NA_STEP_ROWS == 0 and rows >= 2 * NA_KH + NA_Q_ROWS * (2 * _na_edge_tiles() + 1)
    d_na = NA_HEADS * NA_HEAD_DIM
    n_pairs = d_na // LANES
    steps = rows // NA_STEP_ROWS
    tqs = NA_STEP_ROWS * GRID_W
    tq, tk = NA_Q_ROWS * GRID_W, NA_BAND * GRID_W
    n_case = bias_tab.shape[0]
    vmem = 2 * (2 * tqs * LANES * 2 + 2 * seq * LANES * 2 + n_case * 2 * tq * tk * 4) + 24 * tq * tk * 4
    return pl.pallas_call(
        functools.partial(_na_kernel, rows=rows),
        grid=(batch, n_pairs, steps),
        in_specs=[
            pl.BlockSpec((tqs, LANES), lambda b, hp, r: (b * steps + r, hp)),
            pl.BlockSpec((seq, LANES), lambda b, hp, r: (b, n_pairs + hp)),
            pl.BlockSpec((seq, LANES), lambda b, hp, r: (b, 2 * n_pairs + hp)),
            pl.BlockSpec((n_case, None, 2 * tq, tk), lambda b, hp, r: (0, hp, 0, 0)),
        ],
        out_specs=pl.BlockSpec((tqs, LANES), lambda b, hp, r: (b * steps + r, hp)),
        out_shape=jax.ShapeDtypeStruct((n, d_na), BF16),
        compiler_params=_params(("parallel", "parallel", "arbitrary"), vmem),
        name="natten",
    )(qkv, qkv, qkv, bias_tab)


def _merge_kernel(x_ref, pa_ref, pb_ref, pc_ref, wg0_ref, wg1_ref, wg2_ref, wa_ref, wb_ref, wc_ref, o_ref):
    x = x_ref[...]
    acc = None
    for br_ref, wg_ref, wbr_ref in ((pa_ref, wg0_ref, wa_ref), (pb_ref, wg1_ref, wb_ref), (pc_ref, wg2_ref, wc_ref)):
        gate = jax.nn.sigmoid(jnp.dot(x, wg_ref[...], preferred_element_type=F32))
        y = jnp.dot(br_ref[...], wbr_ref[...], preferred_element_type=F32)
        acc = gate * y if acc is None else acc + gate * y
    o_ref[...] = acc.astype(BF16)


def _merge(xb, pa, pb, pc, w_in_b, w_a, w_b, w_c, layer):
    n, d = xb.shape
    dbr = pa.shape[1]
    tm, tn = MERGE_TM, MERGE_TN
    gate0 = (w_in_b.shape[2] - N_BRANCH * d) // tn
    per = d // tn
    vmem = 2 * (tm * d * 2 + 3 * tm * dbr * 2 + 3 * d * tn * 2 + 3 * dbr * tn * 2 + tm * tn * 2) + 4 * tm * tn * 4
    gate_spec = lambda b: pl.BlockSpec((None, d, tn), lambda i, j: (layer, 0, gate0 + b * per + j))
    br_spec = pl.BlockSpec((tm, dbr), lambda i, j: (i, 0))
    wbr_spec = pl.BlockSpec((None, dbr, tn), lambda i, j: (layer, 0, j))
    return pl.pallas_call(
        _merge_kernel,
        grid=(n // tm, per),
        in_specs=[pl.BlockSpec((tm, d), lambda i, j: (i, 0)), br_spec, br_spec, br_spec,
                  gate_spec(0), gate_spec(1), gate_spec(2), wbr_spec, wbr_spec, wbr_spec],
        out_specs=pl.BlockSpec((tm, tn), lambda i, j: (i, j)),
        out_shape=jax.ShapeDtypeStruct((n, d), BF16),
        compiler_params=_params(("parallel", "arbitrary"), vmem),
        name="merge",
    )(xb, pa, pb, pc, w_in_b, w_in_b, w_in_b, w_a, w_b, w_c)


def _out_ln_kernel(m_ref, x_ref, w_ref, g_ref, b_ref, of_ref, ob_ref, *, alpha):
    for r in range(m_ref.shape[0] // OUT_ROWS):
        rows = slice(r * OUT_ROWS, (r + 1) * OUT_ROWS)
        y = alpha * x_ref[rows, :] + jnp.dot(m_ref[rows, :], w_ref[...], preferred_element_type=F32)
        y = _layer_norm(y, g_ref[...], b_ref[...])
        of_ref[rows, :] = y
        ob_ref[rows, :] = y.astype(BF16)


def _out_ln(merged, xf, w_out_b, g, b, layer, alpha):
    n, d = xf.shape
    tm = OUT_TM
    vmem = 2 * (tm * d * 2 + tm * d * 4 + d * d * 2 + tm * d * 4 + tm * d * 2) + 4 * OUT_ROWS * d * 4
    row = pl.BlockSpec((tm, d), lambda i: (i, 0))
    vec = pl.BlockSpec((None, 1, d), lambda i: (layer, 0, 0))
    return pl.pallas_call(
        functools.partial(_out_ln_kernel, alpha=alpha),
        grid=(n // tm,),
        in_specs=[row, row, pl.BlockSpec((None, d, d), lambda i: (layer, 0, 0)), vec, vec],
        out_specs=[row, row],
        out_shape=[jax.ShapeDtypeStruct((n, d), F32), jax.ShapeDtypeStruct((n, d), BF16)],
        compiler_params=_params(("parallel",), vmem),
        name="out_ln",
    )(merged, xf, w_out_b, g, b)


def _ffn_kernel(xb_ref, xf_ref, wu_ref, wd_ref, g_ref, b_ref, of_ref, ob_ref, *, alpha):
    f = pl.program_id(1)
    last = pl.num_programs(1) - 1

    def step(first, final):
        for r in range(xb_ref.shape[0] // FFN_ROWS):
            rows = slice(r * FFN_ROWS, (r + 1) * FFN_ROWS)
            h = jnp.maximum(jnp.dot(xb_ref[rows, :], wu_ref[...], preferred_element_type=F32), 0.0)
            upd = jnp.dot((h * h).astype(BF16), wd_ref[...], preferred_element_type=F32)
            acc = (alpha * xf_ref[rows, :] if first else of_ref[rows, :]) + upd
            if final:
                acc = _layer_norm(acc, g_ref[...], b_ref[...])
                ob_ref[rows, :] = acc.astype(BF16)
            of_ref[rows, :] = acc

    @pl.when(f == 0)
    def _():
        step(True, False)

    @pl.when((f > 0) & (f < last))
    def _():
        step(False, False)

    @pl.when(f == last)
    def _():
        step(False, True)


def _ffn(xb, xf, w_up_b, w_down_b, g, b, layer, alpha):
    n, d = xf.shape
    dff = w_up_b.shape[2]
    tm, tf = FFN_TM, FFN_TF
    assert dff // tf >= 2
    vmem = (2 * (tm * d * 2 + tm * d * 4 + d * tf * 2 + tf * d * 2 + tm * d * 4 + tm * d * 2)
            + FFN_ROWS * tf * 6 + 2 * FFN_ROWS * d * 4)
    row = pl.BlockSpec((tm, d), lambda i, f: (i, 0))
    vec = pl.BlockSpec((None, 1, d), lambda i, f: (layer, 0, 0))
    return pl.pallas_call(
        functools.partial(_ffn_kernel, alpha=alpha),
        grid=(n // tm, dff // tf),
        in_specs=[row, row, pl.BlockSpec((None, d, tf), lambda i, f: (layer, 0, f)),
                  pl.BlockSpec((None, tf, d), lambda i, f: (layer, f, 0)), vec, vec],
        out_specs=[row, row],
        out_shape=[jax.ShapeDtypeStruct((n, d), F32), jax.ShapeDtypeStruct((n, d), BF16)],
        compiler_params=_params(("parallel", "arbitrary"), vmem),
        name="ffn",
    )(xb, xf, w_up_b, w_down_b, g, b)


def kernel(x_prompt, x_sample, w_in, w_pool, s_pool, sgu_ln_g, sgu_ln_b, w_s, b_s, rpb, w_br_pool, w_br_sgu,
           w_br_na, w_out, ln1_g, ln1_b, w_up, w_down, ln2_g, ln2_b):
    depth = w_in.shape[0]
    alpha = float((2 * depth) ** 0.25)
    w_in_b, w_pool_b, w_s_b = w_in.astype(BF16), w_pool.astype(BF16), w_s.astype(BF16)
    w_a, w_b, w_c = w_br_pool.astype(BF16), w_br_sgu.astype(BF16), w_br_na.astype(BF16)
    w_out_b, w_up_b, w_down_b = w_out.astype(BF16), w_up.astype(BF16), w_down.astype(BF16)
    vec = lambda v: v[:, None, :]
    sgu_g, sgu_b, s_pool_v, b_s_col = vec(sgu_ln_g), vec(sgu_ln_b), vec(s_pool), b_s[..., None]
    ln1 = (vec(ln1_g), vec(ln1_b))
    ln2 = (vec(ln2_g), vec(ln2_b))
    bias_tabs = [_na_bias_table(rpb[l]) for l in range(depth)]

    outs = []
    for x in (x_prompt, x_sample):
        batch, seq, d = x.shape
        xf = x.reshape(batch * seq, d)
        for l in range(depth):
            if l == 0:
                a, xb = _proj_cast(xf, w_in_b, l)
            else:
                a = _proj(xb, w_in_b, l, 0, 1, "plain", F32, sgu_g, sgu_b)
            ug = _proj(xb, w_in_b, l, 1, 1, "gelu", BF16, sgu_g, sgu_b)
            vn = _proj(xb, w_in_b, l, 2, 1, "gelu_ln", BF16, sgu_g, sgu_b)
            qkv = _proj(xb, w_in_b, l, 3, 3, "plain", BF16, sgu_g, sgu_b)
            pa = _pool(a, w_pool_b, s_pool_v, l, seq)
            pb = _sgu(ug, vn, w_s_b, b_s_col, l)
            pc = _natten(qkv, bias_tabs[l], batch, seq)
            merged = _merge(xb, pa, pb, pc, w_in_b, w_a, w_b, w_c, l)
            xf, xb = _out_ln(merged, xf, w_out_b, *ln1, l, alpha)
            xf, xb = _ffn(xb, xf, w_up_b, w_down_b, *ln2, l, alpha)
        outs.append(xf.reshape(batch, seq, d))
    return tuple(outs)
```

```python
import functools

import jax
import jax.numpy as jnp
import numpy as np
from jax import lax
from jax.experimental import pallas as pl
from jax.experimental.pallas import tpu as pltpu

F32 = jnp.float32
BF16 = jnp.bfloat16

GRID_W = 64
POOL_WINDOWS = (2, 4, 8, 16)
POOL_HALO = max(POOL_WINDOWS) // 2
CHUNK = 128
SGU_GROUPS = 8
NA_HEADS = 16
NA_HEAD_DIM = 64
NA_KH = 8
NA_KW = 16
N_BRANCH = 3
LN_EPS = 1e-5
NEG_BIG = -1e30

LANES = 128
V7X_VMEM_BYTES = 64 * 1024 * 1024
VMEM_COMPILER_RESERVE = 6 * 1024 * 1024
VMEM_HEADROOM = 3 * 1024 * 1024

PROJ_TM = 2048
PROJ_TN = 1024
PROJ_ROWS = 256
POOL_TM = 512
SGU_TM = 512
NA_Q_ROWS = 4
NA_BAND = NA_Q_ROWS + NA_KH - 1
NA_STEP_ROWS = 64
MERGE_TM = 1024
MERGE_TN = 512
OUT_TM = 512
OUT_ROWS = 128
FFN_TM = 512
FFN_TF = 1024
CAST_TM = 1024


def _params(semantics, vmem_estimate):
    limit = min(int(vmem_estimate) + VMEM_COMPILER_RESERVE, V7X_VMEM_BYTES - VMEM_HEADROOM)
    return pltpu.CompilerParams(dimension_semantics=semantics, vmem_limit_bytes=limit)


def _layer_norm(y, g, b):
    mu = jnp.mean(y, axis=-1, keepdims=True)
    d = y - mu
    var = jnp.mean(d * d, axis=-1, keepdims=True)
    return d * lax.rsqrt(var + LN_EPS) * g + b


def _proj_kernel(x_ref, w_ref, g_ref, b_ref, o_ref, *, mode):
    for r in range(x_ref.shape[0] // PROJ_ROWS):
        rows = slice(r * PROJ_ROWS, (r + 1) * PROJ_ROWS)
        acc = jnp.dot(x_ref[rows, :], w_ref[...], preferred_element_type=F32)
        if mode == "gelu":
            acc = jax.nn.gelu(acc)
        elif mode == "gelu_ln":
            acc = _layer_norm(jax.nn.gelu(acc), g_ref[...], b_ref[...])
        o_ref[rows, :] = acc.astype(o_ref.dtype)


def _proj(xb, w_in_b, layer, seg0, nseg, mode, out_dtype, ln_g, ln_b):
    n, d = xb.shape
    tm, tn = PROJ_TM, PROJ_TN
    osz = jnp.dtype(out_dtype).itemsize
    vmem = 2 * (tm * d * 2 + d * tn * 2 + tm * tn * osz) + 4 * PROJ_ROWS * tn * 4
    return pl.pallas_call(
        functools.partial(_proj_kernel, mode=mode),
        grid=(n // tm, nseg),
        in_specs=[
            pl.BlockSpec((tm, d), lambda i, j: (i, 0)),
            pl.BlockSpec((None, d, tn), lambda i, j: (layer, 0, seg0 + j)),
            pl.BlockSpec((None, 1, tn), lambda i, j: (layer, 0, 0)),
            pl.BlockSpec((None, 1, tn), lambda i, j: (layer, 0, 0)),
        ],
        out_specs=pl.BlockSpec((tm, tn), lambda i, j: (i, j)),
        out_shape=jax.ShapeDtypeStruct((n, nseg * tn), out_dtype),
        compiler_params=_params(("parallel", "arbitrary"), vmem),
        name="proj%d_%s" % (seg0, mode),
    )(xb, w_in_b, ln_g, ln_b)


def _proj_cast_kernel(x_ref, w_ref, a_ref, xb_ref):
    for r in range(x_ref.shape[0] // PROJ_ROWS):
        rows = slice(r * PROJ_ROWS, (r + 1) * PROJ_ROWS)
        xb = x_ref[rows, :].astype(BF16)
        xb_ref[rows, :] = xb
        a_ref[rows, :] = jnp.dot(xb, w_ref[...], preferred_element_type=F32)


def _proj_cast(xf, w_in_b, layer):
    n, d = xf.shape
    tm, tn = CAST_TM, PROJ_TN
    vmem = 2 * (tm * d * 4 + d * tn * 2 + tm * tn * 4 + tm * d * 2) + 4 * PROJ_ROWS * tn * 4
    return pl.pallas_call(
        _proj_cast_kernel,
        grid=(n // tm,),
        in_specs=[pl.BlockSpec((tm, d), lambda i: (i, 0)),
                  pl.BlockSpec((None, d, tn), lambda i: (layer, 0, 0))],
        out_specs=[pl.BlockSpec((tm, tn), lambda i: (i, 0)), pl.BlockSpec((tm, d), lambda i: (i, 0))],
        out_shape=[jax.ShapeDtypeStruct((n, tn), F32), jax.ShapeDtypeStruct((n, d), BF16)],
        compiler_params=_params(("parallel",), vmem),
        name="proj0_cast",
    )(xf, w_in_b)


def _pool_kernel(ap_ref, ac_ref, an_ref, wp_ref, sp_ref, o_ref, buf_ref, *, seq, tm):
    i = pl.program_id(0)
    s0 = (i * tm) % seq
    h = POOL_HALO
    buf_ref[0:h, :] = jnp.where(s0 == 0, 0.0, ap_ref[...])
    buf_ref[h:h + tm, :] = ac_ref[...]
    buf_ref[h + tm:2 * h + tm, :] = jnp.where(s0 + tm == seq, 0.0, an_ref[...])
    pos = s0 + lax.broadcasted_iota(jnp.int32, (tm, 1), 0)
    gc = ac_ref.shape[1] // len(POOL_WINDOWS)
    for gi, w in enumerate(POOL_WINDOWS):
        cols = slice(gi * gc, (gi + 1) * gc)
        tot = buf_ref[h - w // 2:h - w // 2 + tm, cols]
        for dlt in range(-w // 2 + 1, w // 2):
            tot = tot + buf_ref[h + dlt:h + dlt + tm, cols]
        cnt = jnp.minimum(pos + w // 2, seq) - jnp.maximum(pos - w // 2, 0)
        p = tot * (1.0 / cnt.astype(F32)) - ac_ref[:, cols]
        y = jnp.dot(p.astype(BF16), wp_ref[gi], preferred_element_type=F32)
        o_ref[:, cols] = (y * sp_ref[:, cols]).astype(BF16)


def _pool(a, w_pool_b, s_pool, layer, seq):
    n, dp = a.shape
    tm, h = POOL_TM, POOL_HALO
    assert seq % tm == 0 and tm % h == 0
    nb = tm // h
    last = n // h - 1
    vmem = 2 * (tm * dp * 4 + 2 * h * dp * 4 + tm * dp * 2) + 4 * dp * dp + (tm + 2 * h) * dp * 4
    return pl.pallas_call(
        functools.partial(_pool_kernel, seq=seq, tm=tm),
        grid=(n // tm,),
        in_specs=[
            pl.BlockSpec((h, dp), lambda i: (jnp.maximum(i * nb - 1, 0), 0)),
            pl.BlockSpec((tm, dp), lambda i: (i, 0)),
            pl.BlockSpec((h, dp), lambda i: (jnp.minimum((i + 1) * nb, last), 0)),
            pl.BlockSpec((None,) + w_pool_b.shape[1:], lambda i: (layer, 0, 0, 0)),
            pl.BlockSpec((None, 1, dp), lambda i: (layer, 0, 0)),
        ],
        out_specs=pl.BlockSpec((tm, dp), lambda i: (i, 0)),
        out_shape=jax.ShapeDtypeStruct((n, dp), BF16),
        scratch_shapes=[pltpu.VMEM((tm + 2 * h, dp), F32)],
        compiler_params=_params(("parallel",), vmem),
        name="pool",
    )(a, a, a, w_pool_b, s_pool)


def _sgu_kernel(u_ref, v_ref, ws_ref, bs_ref, o_ref):
    tm, ds = u_ref.shape
    gc = ds // SGU_GROUPS
    n_chunks = tm // CHUNK
    for g in range(SGU_GROUPS):
        cols = slice(g * gc, (g + 1) * gc)
        v_all = jnp.concatenate([v_ref[c * CHUNK:(c + 1) * CHUNK, cols] for c in range(n_chunks)], axis=1)
        sg = jnp.dot(ws_ref[g], v_all, preferred_element_type=F32) + bs_ref[g]
        for c in range(n_chunks):
            rows = slice(c * CHUNK, (c + 1) * CHUNK)
            o_ref[rows, cols] = (u_ref[rows, cols].astype(F32) * sg[:, c * gc:(c + 1) * gc]).astype(BF16)


def _sgu(ug, vn, w_s_b, b_s_col, layer):
    n, ds = ug.shape
    tm = SGU_TM
    vmem = 2 * (3 * tm * ds * 2) + 2 * SGU_GROUPS * CHUNK * (CHUNK * 2 + LANES * 4)
    row = pl.BlockSpec((tm, ds), lambda i: (i, 0))
    return pl.pallas_call(
        _sgu_kernel,
        grid=(n // tm,),
        in_specs=[row, row,
                  pl.BlockSpec((None,) + w_s_b.shape[1:], lambda i: (layer, 0, 0, 0)),
                  pl.BlockSpec((None,) + b_s_col.shape[1:], lambda i: (layer, 0, 0, 0))],
        out_specs=row,
        out_shape=jax.ShapeDtypeStruct((n, ds), BF16),
        compiler_params=_params(("parallel",), vmem),
        name="sgu",
    )(ug, vn, w_s_b, b_s_col)


def _na_edge_tiles():
    return -(-(NA_KH // 2) // NA_Q_ROWS)


def _na_bias_table(rpb_l):
    n_edge = _na_edge_tiles()
    rows = 2 * NA_KH + NA_Q_ROWS * (2 * n_edge + 1)
    n_tiles = rows // NA_Q_ROWS
    tiles = list(range(n_edge)) + [n_edge] + list(range(n_tiles - n_edge, n_tiles))
    c = np.arange(GRID_W)
    c0 = np.clip(c - NA_KW // 2, 0, GRID_W - NA_KW)
    col_ok = (c[None, :] >= c0[:, None]) & (c[None, :] < c0[:, None] + NA_KW)
    dc = c[None, :] - c[:, None] + (NA_KW - 1)
    onehot = ((dc[None] == np.arange(2 * NA_KW - 1)[:, None, None]) & col_ok[None]).astype(np.float32)
    toep = jnp.einsum("hrd,dck->hcrk", rpb_l, onehot, precision=lax.Precision.HIGHEST)
    toep = jnp.where(col_ok[None, :, None, :], toep, NEG_BIG)
    pad = NA_BAND
    toep = jnp.pad(toep, ((0, 0), (0, 0), (pad, pad), (0, 0)), constant_values=NEG_BIG)
    toep = toep.reshape(NA_HEADS, GRID_W, -1)
    cases = []
    for rt in tiles:
        rs = rt * NA_Q_ROWS
        b0 = int(np.clip(rs - NA_KH // 2, 0, rows - NA_BAND))
        krow = b0 + np.arange(NA_BAND)
        per_row = []
        for i in range(NA_Q_ROWS):
            qrow = rs + i
            r0 = int(np.clip(qrow - NA_KH // 2, 0, rows - NA_KH))
            ok = np.repeat((krow >= r0) & (krow < r0 + NA_KH), GRID_W)
            start = (b0 - qrow + (NA_KH - 1) + pad) * GRID_W
            blk = toep[:, :, start:start + NA_BAND * GRID_W]
            per_row.append(jnp.where(ok[None, None, :], blk, NEG_BIG))
        cases.append(jnp.stack(per_row, axis=1))
    tab = jnp.stack(cases, axis=0)
    return tab.reshape(len(tiles), NA_HEADS // 2, 2 * NA_Q_ROWS * GRID_W, NA_BAND * GRID_W)


def _na_kernel(q_ref, k_ref, v_ref, bias_ref, o_ref, *, rows):
    tq, tk = NA_Q_ROWS * GRID_W, NA_BAND * GRID_W
    n_sub = NA_STEP_ROWS // NA_Q_ROWS
    n_edge = _na_edge_tiles()
    n_tiles = rows // NA_Q_ROWS
    t0 = pl.program_id(2) * n_sub
    lane = lax.broadcasted_iota(jnp.int32, (1, LANES), 1)
    first_head = lane < NA_HEAD_DIM

    for t in range(n_sub):
        rt = t0 + t
        b0 = jnp.clip(rt * NA_Q_ROWS - NA_KH // 2, 0, rows - NA_BAND)
        koff = pl.multiple_of(b0 * GRID_W, GRID_W)
        kb = k_ref[pl.ds(koff, tk), :]
        vb = v_ref[pl.ds(koff, tk), :]
        q = q_ref[t * tq:(t + 1) * tq, :]
        case = jnp.where(rt < n_edge, rt,
                         jnp.where(rt >= n_tiles - n_edge, rt - (n_tiles - n_edge) + n_edge + 1, n_edge))
        zero = jnp.zeros_like(q)
        q2 = jnp.concatenate([jnp.where(first_head, q, zero), jnp.where(first_head, zero, q)], axis=0)
        q2 = q2 * (NA_HEAD_DIM ** -0.5)
        s = lax.dot_general(q2, kb, (((1,), (1,)), ((), ())), preferred_element_type=F32)
        s = s + bias_ref[case]
        m = jnp.max(s, axis=-1, keepdims=True)
        e = jnp.exp(s - m)
        denom = jnp.sum(e, axis=-1, keepdims=True)
        o2 = jnp.dot(e.astype(BF16), vb, preferred_element_type=F32) * (1.0 / denom)
        o_ref[t * tq:(t + 1) * tq, :] = jnp.where(first_head, o2[:tq], o2[tq:]).astype(BF16)


def _natten(qkv, bias_tab, batch, seq):
    n = qkv.shape[0]
    rows = seq // GRID_W
    assert rows % NA_STEP_ROWS == 0 and rows >= 2 * NA_KH + NA_Q_ROWS * (2 * _na_edge_tiles() + 1)
    d_na = NA_HEADS * NA_HEAD_DIM
    n_pairs = d_na // LANES
    steps = rows // NA_STEP_ROWS
    tqs = NA_STEP_ROWS * GRID_W
    tq, tk = NA_Q_ROWS * GRID_W, NA_BAND * GRID_W
    n_case = bias_tab.shape[0]
    vmem = 2 * (2 * tqs * LANES * 2 + 2 * seq * LANES * 2 + n_case * 2 * tq * tk * 4) + 24 * tq * tk * 4
    return pl.pallas_call(
        functools.partial(_na_kernel, rows=rows),
        grid=(batch, n_pairs, steps),
        in_specs=[
            pl.BlockSpec((tqs, LANES), lambda b, hp, r: (b * steps + r, hp)),
            pl.BlockSpec((seq, LANES), lambda b, hp, r: (b, n_pairs + hp)),
            pl.BlockSpec((seq, LANES), lambda b, hp, r: (b, 2 * n_pairs + hp)),
            pl.BlockSpec((n_case, None, 2 * tq, tk), lambda b, hp, r: (0, hp, 0, 0)),
        ],
        out_specs=pl.BlockSpec((tqs, LANES), lambda b, hp, r: (b * steps + r, hp)),
        out_shape=jax.ShapeDtypeStruct((n, d_na), BF16),
        compiler_params=_params(("parallel", "parallel", "arbitrary"), vmem),
        name="natten",
    )(qkv, qkv, qkv, bias_tab)


def _merge_kernel(x_ref, pa_ref, pb_ref, pc_ref, wg0_ref, wg1_ref, wg2_ref, wa_ref, wb_ref, wc_ref, o_ref):
    x = x_ref[...]
    acc = None
    for br_ref, wg_ref, wbr_ref in ((pa_ref, wg0_ref, wa_ref), (pb_ref, wg1_ref, wb_ref), (pc_ref, wg2_ref, wc_ref)):
        gate = jax.nn.sigmoid(jnp.dot(x, wg_ref[...], preferred_element_type=F32))
        y = jnp.dot(br_ref[...], wbr_ref[...], preferred_element_type=F32)
        acc = gate * y if acc is None else acc + gate * y
    o_ref[...] = acc.astype(BF16)


def _merge(xb, pa, pb, pc, w_in_b, w_a, w_b, w_c, layer):
    n, d = xb.shape
    dbr = pa.shape[1]
    tm, tn = MERGE_TM, MERGE_TN
    gate0 = (w_in_b.shape[2] - N_BRANCH * d) // tn
    per = d // tn
    vmem = 2 * (tm * d * 2 + 3 * tm * dbr * 2 + 3 * d * tn * 2 + 3 * dbr * tn * 2 + tm * tn * 2) + 4 * tm * tn * 4
    gate_spec = lambda b: pl.BlockSpec((None, d, tn), lambda i, j: (layer, 0, gate0 + b * per + j))
    br_spec = pl.BlockSpec((tm, dbr), lambda i, j: (i, 0))
    wbr_spec = pl.BlockSpec((None, dbr, tn), lambda i, j: (layer, 0, j))
    return pl.pallas_call(
        _merge_kernel,
        grid=(n // tm, per),
        in_specs=[pl.BlockSpec((tm, d), lambda i, j: (i, 0)), br_spec, br_spec, br_spec,
                  gate_spec(0), gate_spec(1), gate_spec(2), wbr_spec, wbr_spec, wbr_spec],
        out_specs=pl.BlockSpec((tm, tn), lambda i, j: (i, j)),
        out_shape=jax.ShapeDtypeStruct((n, d), BF16),
        compiler_params=_params(("parallel", "arbitrary"), vmem),
        name="merge",
    )(xb, pa, pb, pc, w_in_b, w_in_b, w_in_b, w_a, w_b, w_c)


def _out_ln_kernel(m_ref, x_ref, w_ref, g_ref, b_ref, of_ref, ob_ref, *, alpha):
    for r in range(m_ref.shape[0] // OUT_ROWS):
        rows = slice(r * OUT_ROWS, (r + 1) * OUT_ROWS)
        y = alpha * x_ref[rows, :] + jnp.dot(m_ref[rows, :], w_ref[...], preferred_element_type=F32)
        y = _layer_norm(y, g_ref[...], b_ref[...])
        of_ref[rows, :] = y
        ob_ref[rows, :] = y.astype(BF16)


def _out_ln(merged, xf, w_out_b, g, b, layer, alpha):
    n, d = xf.shape
    tm = OUT_TM
    vmem = 2 * (tm * d * 2 + tm * d * 4 + d * d * 2 + tm * d * 4 + tm * d * 2) + 4 * OUT_ROWS * d * 4
    row = pl.BlockSpec((tm, d), lambda i: (i, 0))
    vec = pl.BlockSpec((None, 1, d), lambda i: (layer, 0, 0))
    return pl.pallas_call(
        functools.partial(_out_ln_kernel, alpha=alpha),
        grid=(n // tm,),
        in_specs=[row, row, pl.BlockSpec((None, d, d), lambda i: (layer, 0, 0)), vec, vec],
        out_specs=[row, row],
        out_shape=[jax.ShapeDtypeStruct((n, d), F32), jax.ShapeDtypeStruct((n, d), BF16)],
        compiler_params=_params(("parallel",), vmem),
        name="out_ln",
    )(merged, xf, w_out_b, g, b)


def _ffn_kernel(xb_ref, xf_ref, wu_ref, wd_ref, g_ref, b_ref, of_ref, ob_ref, *, alpha):
    f = pl.program_id(1)

    @pl.when(f == 0)
    def _():
        of_ref[...] = alpha * xf_ref[...]

    h = jnp.maximum(jnp.dot(xb_ref[...], wu_ref[...], preferred_element_type=F32), 0.0)
    of_ref[...] += jnp.dot((h * h).astype(BF16), wd_ref[...], preferred_element_type=F32)

    @pl.when(f == pl.num_programs(1) - 1)
    def _():
        y = _layer_norm(of_ref[...], g_ref[...], b_ref[...])
        of_ref[...] = y
        ob_ref[...] = y.astype(BF16)


def _ffn(xb, xf, w_up_b, w_down_b, g, b, layer, alpha):
    n, d = xf.shape
    dff = w_up_b.shape[2]
    tm, tf = FFN_TM, FFN_TF
    vmem = 2 * (tm * d * 2 + tm * d * 4 + d * tf * 2 + tf * d * 2 + tm * d * 4 + tm * d * 2) + 3 * tm * tf * 4
    row = pl.BlockSpec((tm, d), lambda i, f: (i, 0))
    vec = pl.BlockSpec((None, 1, d), lambda i, f: (layer, 0, 0))
    return pl.pallas_call(
        functools.partial(_ffn_kernel, alpha=alpha),
        grid=(n // tm, dff // tf),
        in_specs=[row, row, pl.BlockSpec((None, d, tf), lambda i, f: (layer, 0, f)),
                  pl.BlockSpec((None, tf, d), lambda i, f: (layer, f, 0)), vec, vec],
        out_specs=[row, row],
        out_shape=[jax.ShapeDtypeStruct((n, d), F32), jax.ShapeDtypeStruct((n, d), BF16)],
        compiler_params=_params(("parallel", "arbitrary"), vmem),
        name="ffn",
    )(xb, xf, w_up_b, w_down_b, g, b)


def kernel(x_prompt, x_sample, w_in, w_pool, s_pool, sgu_ln_g, sgu_ln_b, w_s, b_s, rpb, w_br_pool, w_br_sgu,
           w_br_na, w_out, ln1_g, ln1_b, w_up, w_down, ln2_g, ln2_b):
    depth = w_in.shape[0]
    alpha = float((2 * depth) ** 0.25)
    w_in_b, w_pool_b, w_s_b = w_in.astype(BF16), w_pool.astype(BF16), w_s.astype(BF16)
    w_a, w_b, w_c = w_br_pool.astype(BF16), w_br_sgu.astype(BF16), w_br_na.astype(BF16)
    w_out_b, w_up_b, w_down_b = w_out.astype(BF16), w_up.astype(BF16), w_down.astype(BF16)
    vec = lambda v: v[:, None, :]
    sgu_g, sgu_b, s_pool_v, b_s_col = vec(sgu_ln_g), vec(sgu_ln_b), vec(s_pool), b_s[..., None]
    ln1 = (vec(ln1_g), vec(ln1_b))
    ln2 = (vec(ln2_g), vec(ln2_b))
    bias_tabs = [_na_bias_table(rpb[l]) for l in range(depth)]

    outs = []
    for x in (x_prompt, x_sample):
        batch, seq, d = x.shape
        xf = x.reshape(batch * seq, d)
        for l in range(depth):
            if l == 0:
                a, xb = _proj_cast(xf, w_in_b, l)
            else:
                a = _proj(xb, w_in_b, l, 0, 1, "plain", F32, sgu_g, sgu_b)
            ug = _proj(xb, w_in_b, l, 1, 1, "gelu", BF16, sgu_g, sgu_b)
            vn = _proj(xb, w_in_b, l, 2, 1, "gelu_ln", BF16, sgu_g, sgu_b)
            qkv = _proj(xb, w_in_b, l, 3, 3, "plain", BF16, sgu_g, sgu_b)
            pa = _pool(a, w_pool_b, s_pool_v, l, seq)
            pb = _sgu(ug, vn, w_s_b, b_s_col, l)
            pc = _natten(qkv, bias_tabs[l], batch, seq)
            merged = _merge(xb, pa, pb, pc, w_in_b, w_a, w_b, w_c, l)
            xf, xb = _out_ln(merged, xf, w_out_b, *ln1, l, alpha)
            xf, xb = _ffn(xb, xf, w_up_b, w_down_b, *ln2, l, alpha)
        outs.append(xf.reshape(batch, seq, d))
    return tuple(outs)
```

```python
import functools

import jax
import jax.numpy as jnp
import numpy as np
from jax import lax
from jax.experimental import pallas as pl
from jax.experimental.pallas import tpu as pltpu

F32 = jnp.float32
BF16 = jnp.bfloat16

GRID_W = 64
POOL_WINDOWS = (2, 4, 8, 16)
POOL_HALO = max(POOL_WINDOWS) // 2
CHUNK = 128
SGU_GROUPS = 8
NA_HEADS = 16
NA_HEAD_DIM = 64
NA_KH = 8
NA_KW = 16
N_BRANCH = 3
LN_EPS = 1e-5
NEG_BIG = -1e30

LANES = 128
V7X_VMEM_BYTES = 64 * 1024 * 1024
VMEM_COMPILER_RESERVE = 6 * 1024 * 1024
VMEM_HEADROOM = 3 * 1024 * 1024

PROJ_TM = 2048
PROJ_TN = 1024
PROJ_ROWS = 256
POOL_TM = 512
SGU_TM = 512
NA_Q_ROWS = 4
NA_BAND = NA_Q_ROWS + NA_KH - 1
NA_STEP_ROWS = 64
MERGE_TM = 1024
MERGE_TN = 512
OUT_TM = 512
OUT_ROWS = 128
FFN_TM = 512
FFN_TF = 1024
CAST_TM = 1024


def _params(semantics, vmem_estimate):
    limit = min(int(vmem_estimate) + VMEM_COMPILER_RESERVE, V7X_VMEM_BYTES - VMEM_HEADROOM)
    return pltpu.CompilerParams(dimension_semantics=semantics, vmem_limit_bytes=limit)


def _snake(i, j, n):
    return jnp.where(i % 2 == 0, j, n - 1 - j)


def _layer_norm(y, g, b):
    mu = jnp.mean(y, axis=-1, keepdims=True)
    d = y - mu
    var = jnp.mean(d * d, axis=-1, keepdims=True)
    return d * lax.rsqrt(var + LN_EPS) * g + b


def _proj_kernel(x_ref, w_ref, g_ref, b_ref, o_ref, *, mode):
    for r in range(x_ref.shape[0] // PROJ_ROWS):
        rows = slice(r * PROJ_ROWS, (r + 1) * PROJ_ROWS)
        acc = jnp.dot(x_ref[rows, :], w_ref[...], preferred_element_type=F32)
        if mode == "gelu":
            acc = jax.nn.gelu(acc)
        elif mode == "gelu_ln":
            acc = _layer_norm(jax.nn.gelu(acc), g_ref[...], b_ref[...])
        o_ref[rows, :] = acc.astype(o_ref.dtype)


def _proj(xb, w_in_b, layer, seg0, nseg, mode, out_dtype, ln_g, ln_b):
    n, d = xb.shape
    tm, tn = PROJ_TM, PROJ_TN
    osz = jnp.dtype(out_dtype).itemsize
    vmem = 2 * (tm * d * 2 + d * tn * 2 + tm * tn * osz) + 4 * PROJ_ROWS * tn * 4
    return pl.pallas_call(
        functools.partial(_proj_kernel, mode=mode),
        grid=(n // tm, nseg),
        in_specs=[
            pl.BlockSpec((tm, d), lambda i, j: (i, 0)),
            pl.BlockSpec((None, d, tn), lambda i, j: (layer, 0, seg0 + _snake(i, j, nseg))),
            pl.BlockSpec((None, 1, tn), lambda i, j: (layer, 0, 0)),
            pl.BlockSpec((None, 1, tn), lambda i, j: (layer, 0, 0)),
        ],
        out_specs=pl.BlockSpec((tm, tn), lambda i, j: (i, _snake(i, j, nseg))),
        out_shape=jax.ShapeDtypeStruct((n, nseg * tn), out_dtype),
        compiler_params=_params(("parallel", "arbitrary"), vmem),
        name="proj%d_%s" % (seg0, mode),
    )(xb, w_in_b, ln_g, ln_b)


def _proj_cast_kernel(x_ref, w_ref, a_ref, xb_ref):
    for r in range(x_ref.shape[0] // PROJ_ROWS):
        rows = slice(r * PROJ_ROWS, (r + 1) * PROJ_ROWS)
        xb = x_ref[rows, :].astype(BF16)
        xb_ref[rows, :] = xb
        a_ref[rows, :] = jnp.dot(xb, w_ref[...], preferred_element_type=F32)


def _proj_cast(xf, w_in_b, layer):
    n, d = xf.shape
    tm, tn = CAST_TM, PROJ_TN
    vmem = 2 * (tm * d * 4 + d * tn * 2 + tm * tn * 4 + tm * d * 2) + 4 * PROJ_ROWS * tn * 4
    return pl.pallas_call(
        _proj_cast_kernel,
        grid=(n // tm,),
        in_specs=[pl.BlockSpec((tm, d), lambda i: (i, 0)),
                  pl.BlockSpec((None, d, tn), lambda i: (layer, 0, 0))],
        out_specs=[pl.BlockSpec((tm, tn), lambda i: (i, 0)), pl.BlockSpec((tm, d), lambda i: (i, 0))],
        out_shape=[jax.ShapeDtypeStruct((n, tn), F32), jax.ShapeDtypeStruct((n, d), BF16)],
        compiler_params=_params(("parallel",), vmem),
        name="proj0_cast",
    )(xf, w_in_b)


def _pool_kernel(ap_ref, ac_ref, an_ref, wp_ref, sp_ref, o_ref, buf_ref, *, seq, tm):
    i = pl.program_id(0)
    s0 = (i * tm) % seq
    h = POOL_HALO
    buf_ref[0:h, :] = jnp.where(s0 == 0, 0.0, ap_ref[...])
    buf_ref[h:h + tm, :] = ac_ref[...]
    buf_ref[h + tm:2 * h + tm, :] = jnp.where(s0 + tm == seq, 0.0, an_ref[...])
    pos = s0 + lax.broadcasted_iota(jnp.int32, (tm, 1), 0)
    gc = ac_ref.shape[1] // len(POOL_WINDOWS)
    for gi, w in enumerate(POOL_WINDOWS):
        cols = slice(gi * gc, (gi + 1) * gc)
        tot = buf_ref[h - w // 2:h - w // 2 + tm, cols]
        for dlt in range(-w // 2 + 1, w // 2):
            tot = tot + buf_ref[h + dlt:h + dlt + tm, cols]
        cnt = jnp.minimum(pos + w // 2, seq) - jnp.maximum(pos - w // 2, 0)
        p = tot * (1.0 / cnt.astype(F32)) - ac_ref[:, cols]
        y = jnp.dot(p.astype(BF16), wp_ref[gi], preferred_element_type=F32)
        o_ref[:, cols] = (y * sp_ref[:, cols]).astype(BF16)


def _pool(a, w_pool_b, s_pool, layer, seq):
    n, dp = a.shape
    tm, h = POOL_TM, POOL_HALO
    assert seq % tm == 0 and tm % h == 0
    nb = tm // h
    last = n // h - 1
    vmem = 2 * (tm * dp * 4 + 2 * h * dp * 4 + tm * dp * 2) + 4 * dp * dp + (tm + 2 * h) * dp * 4
    return pl.pallas_call(
        functools.partial(_pool_kernel, seq=seq, tm=tm),
        grid=(n // tm,),
        in_specs=[
            pl.BlockSpec((h, dp), lambda i: (jnp.maximum(i * nb - 1, 0), 0)),
            pl.BlockSpec((tm, dp), lambda i: (i, 0)),
            pl.BlockSpec((h, dp), lambda i: (jnp.minimum((i + 1) * nb, last), 0)),
            pl.BlockSpec((None,) + w_pool_b.shape[1:], lambda i: (layer, 0, 0, 0)),
            pl.BlockSpec((None, 1, dp), lambda i: (layer, 0, 0)),
        ],
        out_specs=pl.BlockSpec((tm, dp), lambda i: (i, 0)),
        out_shape=jax.ShapeDtypeStruct((n, dp), BF16),
        scratch_shapes=[pltpu.VMEM((tm + 2 * h, dp), F32)],
        compiler_params=_params(("parallel",), vmem),
        name="pool",
    )(a, a, a, w_pool_b, s_pool)


def _sgu_kernel(u_ref, v_ref, ws_ref, bs_ref, o_ref):
    tm, ds = u_ref.shape
    gc = ds // SGU_GROUPS
    n_chunks = tm // CHUNK
    for g in range(SGU_GROUPS):
        cols = slice(g * gc, (g + 1) * gc)
        v_all = jnp.concatenate([v_ref[c * CHUNK:(c + 1) * CHUNK, cols] for c in range(n_chunks)], axis=1)
        sg = jnp.dot(ws_ref[g], v_all, preferred_element_type=F32) + bs_ref[g]
        for c in range(n_chunks):
            rows = slice(c * CHUNK, (c + 1) * CHUNK)
            o_ref[rows, cols] = (u_ref[rows, cols].astype(F32) * sg[:, c * gc:(c + 1) * gc]).astype(BF16)


def _sgu(ug, vn, w_s_b, b_s_col, layer):
    n, ds = ug.shape
    tm = SGU_TM
    vmem = 2 * (3 * tm * ds * 2) + 2 * SGU_GROUPS * CHUNK * (CHUNK * 2 + LANES * 4)
    row = pl.BlockSpec((tm, ds), lambda i: (i, 0))
    return pl.pallas_call(
        _sgu_kernel,
        grid=(n // tm,),
        in_specs=[row, row,
                  pl.BlockSpec((None,) + w_s_b.shape[1:], lambda i: (layer, 0, 0, 0)),
                  pl.BlockSpec((None,) + b_s_col.shape[1:], lambda i: (layer, 0, 0, 0))],
        out_specs=row,
        out_shape=jax.ShapeDtypeStruct((n, ds), BF16),
        compiler_params=_params(("parallel",), vmem),
        name="sgu",
    )(ug, vn, w_s_b, b_s_col)


def _na_edge_tiles():
    return -(-(NA_KH // 2) // NA_Q_ROWS)


def _na_bias_table(rpb_l):
    n_edge = _na_edge_tiles()
    rows = 2 * NA_KH + NA_Q_ROWS * (2 * n_edge + 1)
    n_tiles = rows // NA_Q_ROWS
    tiles = list(range(n_edge)) + [n_edge] + list(range(n_tiles - n_edge, n_tiles))
    c = np.arange(GRID_W)
    c0 = np.clip(c - NA_KW // 2, 0, GRID_W - NA_KW)
    col_ok = (c[None, :] >= c0[:, None]) & (c[None, :] < c0[:, None] + NA_KW)
    dc = c[None, :] - c[:, None] + (NA_KW - 1)
    onehot = ((dc[None] == np.arange(2 * NA_KW - 1)[:, None, None]) & col_ok[None]).astype(np.float32)
    toep = jnp.einsum("hrd,dck->hcrk", rpb_l, onehot, precision=lax.Precision.HIGHEST)
    toep = jnp.where(col_ok[None, :, None, :], toep, NEG_BIG)
    pad = NA_BAND
    toep = jnp.pad(toep, ((0, 0), (0, 0), (pad, pad), (0, 0)), constant_values=NEG_BIG)
    toep = toep.reshape(NA_HEADS, GRID_W, -1)
    cases = []
    for rt in tiles:
        rs = rt * NA_Q_ROWS
        b0 = int(np.clip(rs - NA_KH // 2, 0, rows - NA_BAND))
        krow = b0 + np.arange(NA_BAND)
        per_row = []
        for i in range(NA_Q_ROWS):
            qrow = rs + i
            r0 = int(np.clip(qrow - NA_KH // 2, 0, rows - NA_KH))
            ok = np.repeat((krow >= r0) & (krow < r0 + NA_KH), GRID_W)
            start = (b0 - qrow + (NA_KH - 1) + pad) * GRID_W
            blk = toep[:, :, start:start + NA_BAND * GRID_W]
            per_row.append(jnp.where(ok[None, None, :], blk, NEG_BIG))
        cases.append(jnp.stack(per_row, axis=1))
    tab = jnp.stack(cases, axis=0)
    return tab.reshape(len(tiles), NA_HEADS // 2, 2 * NA_Q_ROWS * GRID_W, NA_BAND * GRID_W)


def _na_kernel(q_ref, k_ref, v_ref, bias_ref, o_ref, *, rows):
    tq, tk = NA_Q_ROWS * GRID_W, NA_BAND * GRID_W
    n_sub = NA_STEP_ROWS // NA_Q_ROWS
    n_edge = _na_edge_tiles()
    n_tiles = rows // NA_Q_ROWS
    t0 = pl.program_id(2) * n_sub
    lane = lax.broadcasted_iota(jnp.int32, (1, LANES), 1)
    first_head = lane < NA_HEAD_DIM

    for t in range(n_sub):
        rt = t0 + t
        b0 = jnp.clip(rt * NA_Q_ROWS - NA_KH // 2, 0, rows - NA_BAND)
        koff = pl.multiple_of(b0 * GRID_W, GRID_W)
        kb = k_ref[pl.ds(koff, tk), :]
        vb = v_ref[pl.ds(koff, tk), :]
        q = q_ref[t * tq:(t + 1) * tq, :]
        case = jnp.where(rt < n_edge, rt,
                         jnp.where(rt >= n_tiles - n_edge, rt - (n_tiles - n_edge) + n_edge + 1, n_edge))
        zero = jnp.zeros_like(q)
        q2 = jnp.concatenate([jnp.where(first_head, q, zero), jnp.where(first_head, zero, q)], axis=0)
        q2 = q2 * (NA_HEAD_DIM ** -0.5)
        s = lax.dot_general(q2, kb, (((1,), (1,)), ((), ())), preferred_element_type=F32)
        s = s + bias_ref[case]
        m = jnp.max(s, axis=-1, keepdims=True)
        e = jnp.exp(s - m)
        denom = jnp.sum(e, axis=-1, keepdims=True)
        o2 = jnp.dot(e.astype(BF16), vb, preferred_element_type=F32) * (1.0 / denom)
        o_ref[t * tq:(t + 1) * tq, :] = jnp.where(first_head, o2[:tq], o2[tq:]).astype(BF16)


def _natten(qkv, bias_tab, batch, seq):
    n = qkv.shape[0]
    rows = seq // GRID_W
    assert rows % NA_STEP_ROWS == 0 and rows >= 2 * NA_KH + NA_Q_ROWS * (2 * _na_edge_tiles() + 1)
    d_na = NA_HEADS * NA_HEAD_DIM
    n_pairs = d_na // LANES
    steps = rows // NA_STEP_ROWS
    tqs = NA_STEP_ROWS * GRID_W
    tq, tk = NA_Q_ROWS * GRID_W, NA_BAND * GRID_W
    n_case = bias_tab.shape[0]
    vmem = 2 * (2 * tqs * LANES * 2 + 2 * seq * LANES * 2 + n_case * 2 * tq * tk * 4) + 24 * tq * tk * 4
    return pl.pallas_call(
        functools.partial(_na_kernel, rows=rows),
        grid=(batch, n_pairs, steps),
        in_specs=[
            pl.BlockSpec((tqs, LANES), lambda b, hp, r: (b * steps + r, hp)),
            pl.BlockSpec((seq, LANES), lambda b, hp, r: (b, n_pairs + hp)),
            pl.BlockSpec((seq, LANES), lambda b, hp, r: (b, 2 * n_pairs + hp)),
            pl.BlockSpec((n_case, None, 2 * tq, tk), lambda b, hp, r: (0, hp, 0, 0)),
        ],
        out_specs=pl.BlockSpec((tqs, LANES), lambda b, hp, r: (b * steps + r, hp)),
        out_shape=jax.ShapeDtypeStruct((n, d_na), BF16),
        compiler_params=_params(("parallel", "parallel", "arbitrary"), vmem),
        name="natten",
    )(qkv, qkv, qkv, bias_tab)


def _merge_kernel(x_ref, pa_ref, pb_ref, pc_ref, wg0_ref, wg1_ref, wg2_ref, wa_ref, wb_ref, wc_ref, o_ref):
    x = x_ref[...]
    acc = None
    for br_ref, wg_ref, wbr_ref in ((pa_ref, wg0_ref, wa_ref), (pb_ref, wg1_ref, wb_ref), (pc_ref, wg2_ref, wc_ref)):
        gate = jax.nn.sigmoid(jnp.dot(x, wg_ref[...], preferred_element_type=F32))
        y = jnp.dot(br_ref[...], wbr_ref[...], preferred_element_type=F32)
        acc = gate * y if acc is None else acc + gate * y
    o_ref[...] = acc.astype(BF16)


def _merge(xb, pa, pb, pc, w_in_b, w_a, w_b, w_c, layer):
    n, d = xb.shape
    dbr = pa.shape[1]
    tm, tn = MERGE_TM, MERGE_TN
    gate0 = (w_in_b.shape[2] - N_BRANCH * d) // tn
    per = d // tn
    vmem = 2 * (tm * d * 2 + 3 * tm * dbr * 2 + 3 * d * tn * 2 + 3 * dbr * tn * 2 + tm * tn * 2) + 4 * tm * tn * 4
    gate_spec = lambda b: pl.BlockSpec((None, d, tn), lambda i, j: (layer, 0, gate0 + b * per + _snake(i, j, per)))
    br_spec = pl.BlockSpec((tm, dbr), lambda i, j: (i, 0))
    wbr_spec = pl.BlockSpec((None, dbr, tn), lambda i, j: (layer, 0, _snake(i, j, per)))
    return pl.pallas_call(
        _merge_kernel,
        grid=(n // tm, per),
        in_specs=[pl.BlockSpec((tm, d), lambda i, j: (i, 0)), br_spec, br_spec, br_spec,
                  gate_spec(0), gate_spec(1), gate_spec(2), wbr_spec, wbr_spec, wbr_spec],
        out_specs=pl.BlockSpec((tm, tn), lambda i, j: (i, _snake(i, j, per))),
        out_shape=jax.ShapeDtypeStruct((n, d), BF16),
        compiler_params=_params(("parallel", "arbitrary"), vmem),
        name="merge",
    )(xb, pa, pb, pc, w_in_b, w_in_b, w_in_b, w_a, w_b, w_c)


def _out_ln_kernel(m_ref, x_ref, w_ref, g_ref, b_ref, of_ref, ob_ref, *, alpha):
    for r in range(m_ref.shape[0] // OUT_ROWS):
        rows = slice(r * OUT_ROWS, (r + 1) * OUT_ROWS)
        y = alpha * x_ref[rows, :] + jnp.dot(m_ref[rows, :], w_ref[...], preferred_element_type=F32)
        y = _layer_norm(y, g_ref[...], b_ref[...])
        of_ref[rows, :] = y
        ob_ref[rows, :] = y.astype(BF16)


def _out_ln(merged, xf, w_out_b, g, b, layer, alpha):
    n, d = xf.shape
    tm = OUT_TM
    vmem = 2 * (tm * d * 2 + tm * d * 4 + d * d * 2 + tm * d * 4 + tm * d * 2) + 4 * OUT_ROWS * d * 4
    row = pl.BlockSpec((tm, d), lambda i: (i, 0))
    vec = pl.BlockSpec((None, 1, d), lambda i: (layer, 0, 0))
    return pl.pallas_call(
        functools.partial(_out_ln_kernel, alpha=alpha),
        grid=(n // tm,),
        in_specs=[row, row, pl.BlockSpec((None, d, d), lambda i: (layer, 0, 0)), vec, vec],
        out_specs=[row, row],
        out_shape=[jax.ShapeDtypeStruct((n, d), F32), jax.ShapeDtypeStruct((n, d), BF16)],
        compiler_params=_params(("parallel",), vmem),
        name="out_ln",
    )(merged, xf, w_out_b, g, b)


def _ffn_kernel(xb_ref, xf_ref, wu_ref, wd_ref, g_ref, b_ref, of_ref, *maybe_ob_ref, alpha):
    f = pl.program_id(1)

    @pl.when(f == 0)
    def _():
        of_ref[...] = alpha * xf_ref[...]

    h = jnp.maximum(jnp.dot(xb_ref[...], wu_ref[...], preferred_element_type=F32), 0.0)
    of_ref[...] += jnp.dot((h * h).astype(BF16), wd_ref[...], preferred_element_type=F32)

    @pl.when(f == pl.num_programs(1) - 1)
    def _():
        y = _layer_norm(of_ref[...], g_ref[...], b_ref[...])
        of_ref[...] = y
        for ob_ref in maybe_ob_ref:
            ob_ref[...] = y.astype(BF16)


def _ffn(xb, xf, w_up_b, w_down_b, g, b, layer, alpha, with_bf16):
    n, d = xf.shape
    dff = w_up_b.shape[2]
    tm, tf = FFN_TM, FFN_TF
    vmem = 2 * (tm * d * 2 + tm * d * 4 + d * tf * 2 + tf * d * 2 + tm * d * 4 + tm * d * 2) + 3 * tm * tf * 4
    row = pl.BlockSpec((tm, d), lambda i, f: (i, 0))
    vec = pl.BlockSpec((None, 1, d), lambda i, f: (layer, 0, 0))
    n_out = 2 if with_bf16 else 1
    outs = pl.pallas_call(
        functools.partial(_ffn_kernel, alpha=alpha),
        grid=(n // tm, dff // tf),
        in_specs=[row, row, pl.BlockSpec((None, d, tf), lambda i, f: (layer, 0, _snake(i, f, dff // tf))),
                  pl.BlockSpec((None, tf, d), lambda i, f: (layer, _snake(i, f, dff // tf), 0)), vec, vec],
        out_specs=[row, row][:n_out],
        out_shape=[jax.ShapeDtypeStruct((n, d), F32), jax.ShapeDtypeStruct((n, d), BF16)][:n_out],
        compiler_params=_params(("parallel", "arbitrary"), vmem),
        name="ffn",
    )(xb, xf, w_up_b, w_down_b, g, b)
    return (outs[0], outs[1]) if with_bf16 else (outs[0], None)


def kernel(x_prompt, x_sample, w_in, w_pool, s_pool, sgu_ln_g, sgu_ln_b, w_s, b_s, rpb, w_br_pool, w_br_sgu,
           w_br_na, w_out, ln1_g, ln1_b, w_up, w_down, ln2_g, ln2_b):
    depth = w_in.shape[0]
    alpha = float((2 * depth) ** 0.25)
    w_in_b, w_pool_b, w_s_b = w_in.astype(BF16), w_pool.astype(BF16), w_s.astype(BF16)
    w_a, w_b, w_c = w_br_pool.astype(BF16), w_br_sgu.astype(BF16), w_br_na.astype(BF16)
    w_out_b, w_up_b, w_down_b = w_out.astype(BF16), w_up.astype(BF16), w_down.astype(BF16)
    vec = lambda v: v[:, None, :]
    sgu_g, sgu_b, s_pool_v, b_s_col = vec(sgu_ln_g), vec(sgu_ln_b), vec(s_pool), b_s[..., None]
    ln1 = (vec(ln1_g), vec(ln1_b))
    ln2 = (vec(ln2_g), vec(ln2_b))
    bias_tabs = [_na_bias_table(rpb[l]) for l in range(depth)]

    outs = []
    for x in (x_prompt, x_sample):
        batch, seq, d = x.shape
        xf = x.reshape(batch * seq, d)
        for l in range(depth):
            if l == 0:
                a, xb = _proj_cast(xf, w_in_b, l)
            else:
                a = _proj(xb, w_in_b, l, 0, 1, "plain", F32, sgu_g, sgu_b)
            ug = _proj(xb, w_in_b, l, 1, 1, "gelu", BF16, sgu_g, sgu_b)
            vn = _proj(xb, w_in_b, l, 2, 1, "gelu_ln", BF16, sgu_g, sgu_b)
            qkv = _proj(xb, w_in_b, l, 3, 3, "plain", BF16, sgu_g, sgu_b)
            pa = _pool(a, w_pool_b, s_pool_v, l, seq)
            pb = _sgu(ug, vn, w_s_b, b_s_col, l)
            pc = _natten(qkv, bias_tabs[l], batch, seq)
            merged = _merge(xb, pa, pb, pc, w_in_b, w_a, w_b, w_c, l)
            xf, xb = _out_ln(merged, xf, w_out_b, *ln1, l, alpha)
            xf, xb = _ffn(xb, xf, w_up_b, w_down_b, *ln2, l, alpha, with_bf16=l + 1 < depth)
        outs.append(xf.reshape(batch, seq, d))
    return tuple(outs)
```

```python
import functools

import jax
import jax.numpy as jnp
import numpy as np
from jax import lax
from jax.experimental import pallas as pl
from jax.experimental.pallas import tpu as pltpu

F32 = jnp.float32
BF16 = jnp.bfloat16

GRID_W = 64
POOL_WINDOWS = (2, 4, 8, 16)
POOL_HALO = max(POOL_WINDOWS) // 2
CHUNK = 128
SGU_GROUPS = 8
NA_HEADS = 16
NA_HEAD_DIM = 64
NA_KH = 8
NA_KW = 16
N_BRANCH = 3
LN_EPS = 1e-5
NEG_BIG = -1e30

LANES = 128
V7X_VMEM_BYTES = 64 * 1024 * 1024
VMEM_COMPILER_RESERVE = 6 * 1024 * 1024
VMEM_HEADROOM = 3 * 1024 * 1024

PROJ_TM = 2048
PROJ_TN = 1024
PROJ_ROWS = 256
POOL_TM = 512
SGU_TM = 512
NA_Q_ROWS = 4
NA_BAND = NA_Q_ROWS + NA_KH - 1
NA_STEP_ROWS = 64
MERGE_TM = 1024
MERGE_TN = 512
OUT_TM = 512
OUT_ROWS = 128
FFN_TM = 1024
FFN_TF = 512
CAST_TM = 1024


def _params(semantics, vmem_estimate):
    limit = min(int(vmem_estimate) + VMEM_COMPILER_RESERVE, V7X_VMEM_BYTES - VMEM_HEADROOM)
    return pltpu.CompilerParams(dimension_semantics=semantics, vmem_limit_bytes=limit)


def _snake(i, j, n):
    return jnp.where(i % 2 == 0, j, n - 1 - j)


def _layer_norm(y, g, b):
    mu = jnp.mean(y, axis=-1, keepdims=True)
    d = y - mu
    var = jnp.mean(d * d, axis=-1, keepdims=True)
    return d * lax.rsqrt(var + LN_EPS) * g + b


def _proj_kernel(x_ref, w_ref, g_ref, b_ref, o_ref, *, mode):
    for r in range(x_ref.shape[0] // PROJ_ROWS):
        rows = slice(r * PROJ_ROWS, (r + 1) * PROJ_ROWS)
        acc = jnp.dot(x_ref[rows, :], w_ref[...], preferred_element_type=F32)
        if mode == "gelu":
            acc = jax.nn.gelu(acc)
        elif mode == "gelu_ln":
            acc = _layer_norm(jax.nn.gelu(acc), g_ref[...], b_ref[...])
        o_ref[rows, :] = acc.astype(o_ref.dtype)


def _proj(xb, w_in_b, layer, seg0, nseg, mode, out_dtype, ln_g, ln_b):
    n, d = xb.shape
    tm, tn = PROJ_TM, PROJ_TN
    osz = jnp.dtype(out_dtype).itemsize
    vmem = 2 * (tm * d * 2 + d * tn * 2 + tm * tn * osz) + 4 * PROJ_ROWS * tn * 4
    return pl.pallas_call(
        functools.partial(_proj_kernel, mode=mode),
        grid=(n // tm, nseg),
        in_specs=[
            pl.BlockSpec((tm, d), lambda i, j: (i, 0)),
            pl.BlockSpec((None, d, tn), lambda i, j: (layer, 0, seg0 + _snake(i, j, nseg))),
            pl.BlockSpec((None, 1, tn), lambda i, j: (layer, 0, 0)),
            pl.BlockSpec((None, 1, tn), lambda i, j: (layer, 0, 0)),
        ],
        out_specs=pl.BlockSpec((tm, tn), lambda i, j: (i, _snake(i, j, nseg))),
        out_shape=jax.ShapeDtypeStruct((n, nseg * tn), out_dtype),
        compiler_params=_params(("parallel", "arbitrary"), vmem),
        name="proj%d_%s" % (seg0, mode),
    )(xb, w_in_b, ln_g, ln_b)


def _proj_cast_kernel(x_ref, w_ref, a_ref, xb_ref):
    for r in range(x_ref.shape[0] // PROJ_ROWS):
        rows = slice(r * PROJ_ROWS, (r + 1) * PROJ_ROWS)
        xb = x_ref[rows, :].astype(BF16)
        xb_ref[rows, :] = xb
        a_ref[rows, :] = jnp.dot(xb, w_ref[...], preferred_element_type=F32)


def _proj_cast(xf, w_in_b, layer):
    n, d = xf.shape
    tm, tn = CAST_TM, PROJ_TN
    vmem = 2 * (tm * d * 4 + d * tn * 2 + tm * tn * 4 + tm * d * 2) + 4 * PROJ_ROWS * tn * 4
    return pl.pallas_call(
        _proj_cast_kernel,
        grid=(n // tm,),
        in_specs=[pl.BlockSpec((tm, d), lambda i: (i, 0)),
                  pl.BlockSpec((None, d, tn), lambda i: (layer, 0, 0))],
        out_specs=[pl.BlockSpec((tm, tn), lambda i: (i, 0)), pl.BlockSpec((tm, d), lambda i: (i, 0))],
        out_shape=[jax.ShapeDtypeStruct((n, tn), F32), jax.ShapeDtypeStruct((n, d), BF16)],
        compiler_params=_params(("parallel",), vmem),
        name="proj0_cast",
    )(xf, w_in_b)


def _pool_kernel(ap_ref, ac_ref, an_ref, wp_ref, sp_ref, o_ref, buf_ref, *, seq, tm):
    i = pl.program_id(0)
    s0 = (i * tm) % seq
    h = POOL_HALO
    buf_ref[0:h, :] = jnp.where(s0 == 0, 0.0, ap_ref[...])
    buf_ref[h:h + tm, :] = ac_ref[...]
    buf_ref[h + tm:2 * h + tm, :] = jnp.where(s0 + tm == seq, 0.0, an_ref[...])
    pos = s0 + lax.broadcasted_iota(jnp.int32, (tm, 1), 0)
    gc = ac_ref.shape[1] // len(POOL_WINDOWS)
    for gi, w in enumerate(POOL_WINDOWS):
        cols = slice(gi * gc, (gi + 1) * gc)
        tot = buf_ref[h - w // 2:h - w // 2 + tm, cols]
        for dlt in range(-w // 2 + 1, w // 2):
            tot = tot + buf_ref[h + dlt:h + dlt + tm, cols]
        cnt = jnp.minimum(pos + w // 2, seq) - jnp.maximum(pos - w // 2, 0)
        p = tot * (1.0 / cnt.astype(F32)) - ac_ref[:, cols]
        y = jnp.dot(p.astype(BF16), wp_ref[gi], preferred_element_type=F32)
        o_ref[:, cols] = (y * sp_ref[:, cols]).astype(BF16)


def _pool(a, w_pool_b, s_pool, layer, seq):
    n, dp = a.shape
    tm, h = POOL_TM, POOL_HALO
    assert seq % tm == 0 and tm % h == 0
    nb = tm // h
    last = n // h - 1
    vmem = 2 * (tm * dp * 4 + 2 * h * dp * 4 + tm * dp * 2) + 4 * dp * dp + (tm + 2 * h) * dp * 4
    return pl.pallas_call(
        functools.partial(_pool_kernel, seq=seq, tm=tm),
        grid=(n // tm,),
        in_specs=[
            pl.BlockSpec((h, dp), lambda i: (jnp.maximum(i * nb - 1, 0), 0)),
            pl.BlockSpec((tm, dp), lambda i: (i, 0)),
            pl.BlockSpec((h, dp), lambda i: (jnp.minimum((i + 1) * nb, last), 0)),
            pl.BlockSpec((None,) + w_pool_b.shape[1:], lambda i: (layer, 0, 0, 0)),
            pl.BlockSpec((None, 1, dp), lambda i: (layer, 0, 0)),
        ],
        out_specs=pl.BlockSpec((tm, dp), lambda i: (i, 0)),
        out_shape=jax.ShapeDtypeStruct((n, dp), BF16),
        scratch_shapes=[pltpu.VMEM((tm + 2 * h, dp), F32)],
        compiler_params=_params(("parallel",), vmem),
        name="pool",
    )(a, a, a, w_pool_b, s_pool)


def _sgu_kernel(u_ref, v_ref, ws_ref, bs_ref, o_ref):
    tm, ds = u_ref.shape
    gc = ds // SGU_GROUPS
    n_chunks = tm // CHUNK
    for g in range(SGU_GROUPS):
        cols = slice(g * gc, (g + 1) * gc)
        v_all = jnp.concatenate([v_ref[c * CHUNK:(c + 1) * CHUNK, cols] for c in range(n_chunks)], axis=1)
        sg = jnp.dot(ws_ref[g], v_all, preferred_element_type=F32) + bs_ref[g]
        for c in range(n_chunks):
            rows = slice(c * CHUNK, (c + 1) * CHUNK)
            o_ref[rows, cols] = (u_ref[rows, cols].astype(F32) * sg[:, c * gc:(c + 1) * gc]).astype(BF16)


def _sgu(ug, vn, w_s_b, b_s_col, layer):
    n, ds = ug.shape
    tm = SGU_TM
    vmem = 2 * (3 * tm * ds * 2) + 2 * SGU_GROUPS * CHUNK * (CHUNK * 2 + LANES * 4)
    row = pl.BlockSpec((tm, ds), lambda i: (i, 0))
    return pl.pallas_call(
        _sgu_kernel,
        grid=(n // tm,),
        in_specs=[row, row,
                  pl.BlockSpec((None,) + w_s_b.shape[1:], lambda i: (layer, 0, 0, 0)),
                  pl.BlockSpec((None,) + b_s_col.shape[1:], lambda i: (layer, 0, 0, 0))],
        out_specs=row,
        out_shape=jax.ShapeDtypeStruct((n, ds), BF16),
        compiler_params=_params(("parallel",), vmem),
        name="sgu",
    )(ug, vn, w_s_b, b_s_col)


def _na_edge_tiles():
    return -(-(NA_KH // 2) // NA_Q_ROWS)


def _na_bias_table(rpb_l):
    n_edge = _na_edge_tiles()
    rows = 2 * NA_KH + NA_Q_ROWS * (2 * n_edge + 1)
    n_tiles = rows // NA_Q_ROWS
    tiles = list(range(n_edge)) + [n_edge] + list(range(n_tiles - n_edge, n_tiles))
    c = np.arange(GRID_W)
    c0 = np.clip(c - NA_KW // 2, 0, GRID_W - NA_KW)
    col_ok = (c[None, :] >= c0[:, None]) & (c[None, :] < c0[:, None] + NA_KW)
    dc = c[None, :] - c[:, None] + (NA_KW - 1)
    onehot = ((dc[None] == np.arange(2 * NA_KW - 1)[:, None, None]) & col_ok[None]).astype(np.float32)
    toep = jnp.einsum("hrd,dck->hcrk", rpb_l, onehot, precision=lax.Precision.HIGHEST)
    toep = jnp.where(col_ok[None, :, None, :], toep, NEG_BIG)
    pad = NA_BAND
    toep = jnp.pad(toep, ((0, 0), (0, 0), (pad, pad), (0, 0)), constant_values=NEG_BIG)
    toep = toep.reshape(NA_HEADS, GRID_W, -1)
    cases = []
    for rt in tiles:
        rs = rt * NA_Q_ROWS
        b0 = int(np.clip(rs - NA_KH // 2, 0, rows - NA_BAND))
        krow = b0 + np.arange(NA_BAND)
        per_row = []
        for i in range(NA_Q_ROWS):
            qrow = rs + i
            r0 = int(np.clip(qrow - NA_KH // 2, 0, rows - NA_KH))
            ok = np.repeat((krow >= r0) & (krow < r0 + NA_KH), GRID_W)
            start = (b0 - qrow + (NA_KH - 1) + pad) * GRID_W
            blk = toep[:, :, start:start + NA_BAND * GRID_W]
            per_row.append(jnp.where(ok[None, None, :], blk, NEG_BIG))
        cases.append(jnp.stack(per_row, axis=1))
    tab = jnp.stack(cases, axis=0)
    return tab.reshape(len(tiles), NA_HEADS // 2, 2 * NA_Q_ROWS * GRID_W, NA_BAND * GRID_W)


def _na_kernel(q_ref, k_ref, v_ref, bias_ref, o_ref, *, rows):
    tq, tk = NA_Q_ROWS * GRID_W, NA_BAND * GRID_W
    n_sub = NA_STEP_ROWS // NA_Q_ROWS
    n_edge = _na_edge_tiles()
    n_tiles = rows // NA_Q_ROWS
    t0 = pl.program_id(2) * n_sub
    lane = lax.broadcasted_iota(jnp.int32, (1, LANES), 1)
    first_head = lane < NA_HEAD_DIM

    for t in range(n_sub):
        rt = t0 + t
        b0 = jnp.clip(rt * NA_Q_ROWS - NA_KH // 2, 0, rows - NA_BAND)
        koff = pl.multiple_of(b0 * GRID_W, GRID_W)
        kb = k_ref[pl.ds(koff, tk), :]
        vb = v_ref[pl.ds(koff, tk), :]
        q = q_ref[t * tq:(t + 1) * tq, :]
        case = jnp.where(rt < n_edge, rt,
                         jnp.where(rt >= n_tiles - n_edge, rt - (n_tiles - n_edge) + n_edge + 1, n_edge))
        zero = jnp.zeros_like(q)
        q2 = jnp.concatenate([jnp.where(first_head, q, zero), jnp.where(first_head, zero, q)], axis=0)
        q2 = q2 * (NA_HEAD_DIM ** -0.5)
        s = lax.dot_general(q2, kb, (((1,), (1,)), ((), ())), preferred_element_type=F32)
        s = s + bias_ref[case]
        m = jnp.max(s, axis=-1, keepdims=True)
        e = jnp.exp(s - m)
        denom = jnp.sum(e, axis=-1, keepdims=True)
        o2 = jnp.dot(e.astype(BF16), vb, preferred_element_type=F32) * (1.0 / denom)
        o_ref[t * tq:(t + 1) * tq, :] = jnp.where(first_head, o2[:tq], o2[tq:]).astype(BF16)


def _natten(qkv, bias_tab, batch, seq):
    n = qkv.shape[0]
    rows = seq // GRID_W
    assert rows % NA_STEP_ROWS == 0 and rows >= 2 * NA_KH + NA_Q_ROWS * (2 * _na_edge_tiles() + 1)
    d_na = NA_HEADS * NA_HEAD_DIM
    n_pairs = d_na // LANES
    steps = rows // NA_STEP_ROWS
    tqs = NA_STEP_ROWS * GRID_W
    tq, tk = NA_Q_ROWS * GRID_W, NA_BAND * GRID_W
    n_case = bias_tab.shape[0]
    vmem = 2 * (2 * tqs * LANES * 2 + 2 * seq * LANES * 2 + n_case * 2 * tq * tk * 4) + 24 * tq * tk * 4
    return pl.pallas_call(
        functools.partial(_na_kernel, rows=rows),
        grid=(batch, n_pairs, steps),
        in_specs=[
            pl.BlockSpec((tqs, LANES), lambda b, hp, r: (b * steps + r, hp)),
            pl.BlockSpec((seq, LANES), lambda b, hp, r: (b, n_pairs + hp)),
            pl.BlockSpec((seq, LANES), lambda b, hp, r: (b, 2 * n_pairs + hp)),
            pl.BlockSpec((n_case, None, 2 * tq, tk), lambda b, hp, r: (0, hp, 0, 0)),
        ],
        out_specs=pl.BlockSpec((tqs, LANES), lambda b, hp, r: (b * steps + r, hp)),
        out_shape=jax.ShapeDtypeStruct((n, d_na), BF16),
        compiler_params=_params(("parallel", "parallel", "arbitrary"), vmem),
        name="natten",
    )(qkv, qkv, qkv, bias_tab)


def _merge_kernel(x_ref, pa_ref, pb_ref, pc_ref, wg0_ref, wg1_ref, wg2_ref, wa_ref, wb_ref, wc_ref, o_ref):
    x = x_ref[...]
    acc = None
    for br_ref, wg_ref, wbr_ref in ((pa_ref, wg0_ref, wa_ref), (pb_ref, wg1_ref, wb_ref), (pc_ref, wg2_ref, wc_ref)):
        gate = jax.nn.sigmoid(jnp.dot(x, wg_ref[...], preferred_element_type=F32))
        y = jnp.dot(br_ref[...], wbr_ref[...], preferred_element_type=F32)
        acc = gate * y if acc is None else acc + gate * y
    o_ref[...] = acc.astype(BF16)


def _merge(xb, pa, pb, pc, w_in_b, w_a, w_b, w_c, layer):
    n, d = xb.shape
    dbr = pa.shape[1]
    tm, tn = MERGE_TM, MERGE_TN
    gate0 = (w_in_b.shape[2] - N_BRANCH * d) // tn
    per = d // tn
    vmem = 2 * (tm * d * 2 + 3 * tm * dbr * 2 + 3 * d * tn * 2 + 3 * dbr * tn * 2 + tm * tn * 2) + 4 * tm * tn * 4
    gate_spec = lambda b: pl.BlockSpec((None, d, tn), lambda i, j: (layer, 0, gate0 + b * per + _snake(i, j, per)))
    br_spec = pl.BlockSpec((tm, dbr), lambda i, j: (i, 0))
    wbr_spec = pl.BlockSpec((None, dbr, tn), lambda i, j: (layer, 0, _snake(i, j, per)))
    return pl.pallas_call(
        _merge_kernel,
        grid=(n // tm, per),
        in_specs=[pl.BlockSpec((tm, d), lambda i, j: (i, 0)), br_spec, br_spec, br_spec,
                  gate_spec(0), gate_spec(1), gate_spec(2), wbr_spec, wbr_spec, wbr_spec],
        out_specs=pl.BlockSpec((tm, tn), lambda i, j: (i, _snake(i, j, per))),
        out_shape=jax.ShapeDtypeStruct((n, d), BF16),
        compiler_params=_params(("parallel", "arbitrary"), vmem),
        name="merge",
    )(xb, pa, pb, pc, w_in_b, w_in_b, w_in_b, w_a, w_b, w_c)


def _out_ln_kernel(m_ref, x_ref, w_ref, g_ref, b_ref, of_ref, ob_ref, *, alpha):
    for r in range(m_ref.shape[0] // OUT_ROWS):
        rows = slice(r * OUT_ROWS, (r + 1) * OUT_ROWS)
        y = alpha * x_ref[rows, :] + jnp.dot(m_ref[rows, :], w_ref[...], preferred_element_type=F32)
        y = _layer_norm(y, g_ref[...], b_ref[...])
        of_ref[rows, :] = y
        ob_ref[rows, :] = y.astype(BF16)


def _out_ln(merged, xf, w_out_b, g, b, layer, alpha):
    n, d = xf.shape
    tm = OUT_TM
    vmem = 2 * (tm * d * 2 + tm * d * 4 + d * d * 2 + tm * d * 4 + tm * d * 2) + 4 * OUT_ROWS * d * 4
    row = pl.BlockSpec((tm, d), lambda i: (i, 0))
    vec = pl.BlockSpec((None, 1, d), lambda i: (layer, 0, 0))
    return pl.pallas_call(
        functools.partial(_out_ln_kernel, alpha=alpha),
        grid=(n // tm,),
        in_specs=[row, row, pl.BlockSpec((None, d, d), lambda i: (layer, 0, 0)), vec, vec],
        out_specs=[row, row],
        out_shape=[jax.ShapeDtypeStruct((n, d), F32), jax.ShapeDtypeStruct((n, d), BF16)],
        compiler_params=_params(("parallel",), vmem),
        name="out_ln",
    )(merged, xf, w_out_b, g, b)


def _ffn_kernel(xb_ref, xf_ref, wu_ref, wd_ref, g_ref, b_ref, of_ref, *, alpha):
    f = pl.program_id(1)

    @pl.when(f == 0)
    def _():
        of_ref[...] = alpha * xf_ref[...]

    h = jnp.maximum(jnp.dot(xb_ref[...], wu_ref[...], preferred_element_type=F32), 0.0)
    of_ref[...] += jnp.dot((h * h).astype(BF16), wd_ref[...], preferred_element_type=F32)

    @pl.when(f == pl.num_programs(1) - 1)
    def _():
        of_ref[...] = _layer_norm(of_ref[...], g_ref[...], b_ref[...])


def _ffn(xb, xf, w_up_b, w_down_b, g, b, layer, alpha):
    n, d = xf.shape
    dff = w_up_b.shape[2]
    tm, tf = FFN_TM, FFN_TF
    vmem = 2 * (tm * d * 2 + tm * d * 4 + d * tf * 2 + tf * d * 2 + tm * d * 4) + 3 * tm * tf * 4
    row = pl.BlockSpec((tm, d), lambda i, f: (i, 0))
    vec = pl.BlockSpec((None, 1, d), lambda i, f: (layer, 0, 0))
    return pl.pallas_call(
        functools.partial(_ffn_kernel, alpha=alpha),
        grid=(n // tm, dff // tf),
        in_specs=[row, row, pl.BlockSpec((None, d, tf), lambda i, f: (layer, 0, _snake(i, f, dff // tf))),
                  pl.BlockSpec((None, tf, d), lambda i, f: (layer, _snake(i, f, dff // tf), 0)), vec, vec],
        out_specs=row,
        out_shape=jax.ShapeDtypeStruct((n, d), F32),
        compiler_params=_params(("parallel", "arbitrary"), vmem),
        name="ffn",
    )(xb, xf, w_up_b, w_down_b, g, b)


def kernel(x_prompt, x_sample, w_in, w_pool, s_pool, sgu_ln_g, sgu_ln_b, w_s, b_s, rpb, w_br_pool, w_br_sgu,
           w_br_na, w_out, ln1_g, ln1_b, w_up, w_down, ln2_g, ln2_b):
    depth = w_in.shape[0]
    alpha = float((2 * depth) ** 0.25)
    w_in_b, w_pool_b, w_s_b = w_in.astype(BF16), w_pool.astype(BF16), w_s.astype(BF16)
    w_a, w_b, w_c = w_br_pool.astype(BF16), w_br_sgu.astype(BF16), w_br_na.astype(BF16)
    w_out_b, w_up_b, w_down_b = w_out.astype(BF16), w_up.astype(BF16), w_down.astype(BF16)
    vec = lambda v: v[:, None, :]
    sgu_g, sgu_b, s_pool_v, b_s_col = vec(sgu_ln_g), vec(sgu_ln_b), vec(s_pool), b_s[..., None]
    ln1 = (vec(ln1_g), vec(ln1_b))
    ln2 = (vec(ln2_g), vec(ln2_b))
    bias_tabs = [_na_bias_table(rpb[l]) for l in range(depth)]

    outs = []
    for x in (x_prompt, x_sample):
        batch, seq, d = x.shape
        xf = x.reshape(batch * seq, d)
        for l in range(depth):
            a, xb = _proj_cast(xf, w_in_b, l)
            ug =_proj(xb, w_in_b, l, 1, 1, "gelu", BF16, sgu_g, sgu_b)
            vn = _proj(xb, w_in_b, l, 2, 1, "gelu_ln", BF16, sgu_g, sgu_b)
            qkv = _proj(xb, w_in_b, l, 3, 3, "plain", BF16, sgu_g, sgu_b)
            pa = _pool(a, w_pool_b, s_pool_v, l, seq)
            pb = _sgu(ug, vn, w_s_b, b_s_col, l)
            pc = _natten(qkv, bias_tabs[l], batch, seq)
            merged = _merge(xb, pa, pb, pc, w_in_b, w_a, w_b, w_c, l)
            xf, xb = _out_ln(merged, xf, w_out_b, *ln1, l, alpha)
            xf = _ffn(xb, xf, w_up_b, w_down_b, *ln2, l, alpha)
        outs.append(xf.reshape(batch, seq, d))
    return tuple(outs)
```

```python
import functools

import jax
import jax.numpy as jnp
import numpy as np
from jax import lax
from jax.experimental import pallas as pl
from jax.experimental.pallas import tpu as pltpu

F32 = jnp.float32
BF16 = jnp.bfloat16

GRID_W = 64
POOL_WINDOWS = (2, 4, 8, 16)
POOL_HALO = max(POOL_WINDOWS) // 2
CHUNK = 128
SGU_GROUPS = 8
NA_HEADS = 16
NA_HEAD_DIM = 64
NA_KH = 8
NA_KW = 16
N_BRANCH = 3
LN_EPS = 1e-5
NEG_BIG = -1e30

LANES = 128
V7X_VMEM_BYTES = 64 * 1024 * 1024
VMEM_COMPILER_RESERVE = 6 * 1024 * 1024
VMEM_HEADROOM = 3 * 1024 * 1024

PROJ_TM = 2048
PROJ_TN = 1024
PROJ_ROWS = 256
POOL_TM = 512
SGU_TM = 2048
NA_Q_ROWS = 4
NA_BAND = NA_Q_ROWS + NA_KH - 1
NA_STEP_ROWS = 64
MERGE_TM = 1024
MERGE_TN = 512
OUT_TM = 512
OUT_ROWS = 128
FFN_TM = 512
FFN_TF = 1024
CAST_TM = 1024


def _params(semantics, vmem_estimate):
    limit = min(int(vmem_estimate) + VMEM_COMPILER_RESERVE, V7X_VMEM_BYTES - VMEM_HEADROOM)
    return pltpu.CompilerParams(dimension_semantics=semantics, vmem_limit_bytes=limit)


def _layer_norm(y, g, b):
    mu = jnp.mean(y, axis=-1, keepdims=True)
    d = y - mu
    var = jnp.mean(d * d, axis=-1, keepdims=True)
    return d * lax.rsqrt(var + LN_EPS) * g + b


def _proj_kernel(x_ref, w_ref, g_ref, b_ref, o_ref, *, mode):
    for r in range(x_ref.shape[0] // PROJ_ROWS):
        rows = slice(r * PROJ_ROWS, (r + 1) * PROJ_ROWS)
        acc = jnp.dot(x_ref[rows, :], w_ref[...], preferred_element_type=F32)
        if mode == "gelu":
            acc = jax.nn.gelu(acc)
        elif mode == "gelu_ln":
            acc = _layer_norm(jax.nn.gelu(acc), g_ref[...], b_ref[...])
        o_ref[rows, :] = acc.astype(o_ref.dtype)


def _proj(xb, w_in_b, layer, seg0, nseg, mode, out_dtype, ln_g, ln_b):
    n, d = xb.shape
    tm, tn = PROJ_TM, PROJ_TN
    osz = jnp.dtype(out_dtype).itemsize
    vmem = 2 * (tm * d * 2 + d * tn * 2 + tm * tn * osz) + 4 * PROJ_ROWS * tn * 4
    return pl.pallas_call(
        functools.partial(_proj_kernel, mode=mode),
        grid=(n // tm, nseg),
        in_specs=[
            pl.BlockSpec((tm, d), lambda i, j: (i, 0)),
            pl.BlockSpec((None, d, tn), lambda i, j: (layer, 0, seg0 + j)),
            pl.BlockSpec((None, 1, tn), lambda i, j: (layer, 0, 0)),
            pl.BlockSpec((None, 1, tn), lambda i, j: (layer, 0, 0)),
        ],
        out_specs=pl.BlockSpec((tm, tn), lambda i, j: (i, j)),
        out_shape=jax.ShapeDtypeStruct((n, nseg * tn), out_dtype),
        compiler_params=_params(("parallel", "arbitrary"), vmem),
        name="proj%d_%s" % (seg0, mode),
    )(xb, w_in_b, ln_g, ln_b)


def _proj_cast_kernel(x_ref, w_ref, a_ref, xb_ref):
    for r in range(x_ref.shape[0] // PROJ_ROWS):
        rows = slice(r * PROJ_ROWS, (r + 1) * PROJ_ROWS)
        xb = x_ref[rows, :].astype(BF16)
        xb_ref[rows, :] = xb
        a_ref[rows, :] = jnp.dot(xb, w_ref[...], preferred_element_type=F32)


def _proj_cast(xf, w_in_b, layer):
    n, d = xf.shape
    tm, tn = CAST_TM, PROJ_TN
    vmem = 2 * (tm * d * 4 + d * tn * 2 + tm * tn * 4 + tm * d * 2) + 4 * PROJ_ROWS * tn * 4
    return pl.pallas_call(
        _proj_cast_kernel,
        grid=(n // tm,),
        in_specs=[pl.BlockSpec((tm, d), lambda i: (i, 0)),
                  pl.BlockSpec((None, d, tn), lambda i: (layer, 0, 0))],
        out_specs=[pl.BlockSpec((tm, tn), lambda i: (i, 0)), pl.BlockSpec((tm, d), lambda i: (i, 0))],
        out_shape=[jax.ShapeDtypeStruct((n, tn), F32), jax.ShapeDtypeStruct((n, d), BF16)],
        compiler_params=_params(("parallel",), vmem),
        name="proj0_cast",
    )(xf, w_in_b)


def _pool_kernel(ap_ref, ac_ref, an_ref, wp_ref, sp_ref, o_ref, buf_ref, *, seq, tm):
    i = pl.program_id(0)
    s0 = (i * tm) % seq
    h = POOL_HALO
    buf_ref[0:h, :] = jnp.where(s0 == 0, 0.0, ap_ref[...])
    buf_ref[h:h + tm, :] = ac_ref[...]
    buf_ref[h + tm:2 * h + tm, :] = jnp.where(s0 + tm == seq, 0.0, an_ref[...])
    pos = s0 + lax.broadcasted_iota(jnp.int32, (tm, 1), 0)
    gc = ac_ref.shape[1] // len(POOL_WINDOWS)
    for gi, w in enumerate(POOL_WINDOWS):
        cols = slice(gi * gc, (gi + 1) * gc)
        tot = buf_ref[h - w // 2:h - w // 2 + tm, cols]
        for dlt in range(-w // 2 + 1, w // 2):
            tot = tot + buf_ref[h + dlt:h + dlt + tm, cols]
        cnt = jnp.minimum(pos + w // 2, seq) - jnp.maximum(pos - w // 2, 0)
        p = tot * (1.0 / cnt.astype(F32)) - ac_ref[:, cols]
        y = jnp.dot(p.astype(BF16), wp_ref[gi], preferred_element_type=F32)
        o_ref[:, cols] = (y * sp_ref[:, cols]).astype(BF16)


def _pool(a, w_pool_b, s_pool, layer, seq):
    n, dp = a.shape
    tm, h = POOL_TM, POOL_HALO
    assert seq % tm == 0 and tm % h == 0
    nb = tm // h
    last = n // h - 1
    vmem = 2 * (tm * dp * 4 + 2 * h * dp * 4 + tm * dp * 2) + 4 * dp * dp + (tm + 2 * h) * dp * 4
    return pl.pallas_call(
        functools.partial(_pool_kernel, seq=seq, tm=tm),
        grid=(n // tm,),
        in_specs=[
            pl.BlockSpec((h, dp), lambda i: (jnp.maximum(i * nb - 1, 0), 0)),
            pl.BlockSpec((tm, dp), lambda i: (i, 0)),
            pl.BlockSpec((h, dp), lambda i: (jnp.minimum((i + 1) * nb, last), 0)),
            pl.BlockSpec((None,) + w_pool_b.shape[1:], lambda i: (layer, 0, 0, 0)),
            pl.BlockSpec((None, 1, dp), lambda i: (layer, 0, 0)),
        ],
        out_specs=pl.BlockSpec((tm, dp), lambda i: (i, 0)),
        out_shape=jax.ShapeDtypeStruct((n, dp), BF16),
        scratch_shapes=[pltpu.VMEM((tm + 2 * h, dp), F32)],
        compiler_params=_params(("parallel",), vmem),
        name="pool",
    )(a, a, a, w_pool_b, s_pool)


def _sgu_kernel(u_ref, v_ref, ws_ref, bs_ref, o_ref):
    tm, ds = u_ref.shape
    gc = ds // SGU_GROUPS
    n_chunks = tm // CHUNK
    for g in range(SGU_GROUPS):
        cols = slice(g * gc, (g + 1) * gc)
        v_all = jnp.concatenate([v_ref[c * CHUNK:(c + 1) * CHUNK, cols] for c in range(n_chunks)], axis=1)
        sg = jnp.dot(ws_ref[g], v_all, preferred_element_type=F32) + bs_ref[g]
        for c in range(n_chunks):
            rows = slice(c * CHUNK, (c + 1) * CHUNK)
            o_ref[rows, cols] = (u_ref[rows, cols].astype(F32) * sg[:, c * gc:(c + 1) * gc]).astype(BF16)


def _sgu(ug, vn, w_s_b, b_s_col, layer):
    n, ds = ug.shape
    tm = SGU_TM
    vmem = 2 * (3 * tm * ds * 2) + 2 * SGU_GROUPS * CHUNK * (CHUNK * 2 + LANES * 4)
    row = pl.BlockSpec((tm, ds), lambda i: (i, 0))
    return pl.pallas_call(
        _sgu_kernel,
        grid=(n // tm,),
        in_specs=[row, row,
                  pl.BlockSpec((None,) + w_s_b.shape[1:], lambda i: (layer, 0, 0, 0)),
                  pl.BlockSpec((None,) + b_s_col.shape[1:], lambda i: (layer, 0, 0, 0))],
        out_specs=row,
        out_shape=jax.ShapeDtypeStruct((n, ds), BF16),
        compiler_params=_params(("parallel",), vmem),
        name="sgu",
    )(ug, vn, w_s_b, b_s_col)


def _na_edge_tiles():
    return -(-(NA_KH // 2) // NA_Q_ROWS)


def _na_bias_table(rpb_l):
    n_edge = _na_edge_tiles()
    rows = 2 * NA_KH + NA_Q_ROWS * (2 * n_edge + 1)
    n_tiles = rows // NA_Q_ROWS
    tiles = list(range(n_edge)) + [n_edge] + list(range(n_tiles - n_edge, n_tiles))
    c = np.arange(GRID_W)
    c0 = np.clip(c - NA_KW // 2, 0, GRID_W - NA_KW)
    col_ok = (c[None, :] >= c0[:, None]) & (c[None, :] < c0[:, None] + NA_KW)
    dc = c[None, :] - c[:, None] + (NA_KW - 1)
    onehot = ((dc[None] == np.arange(2 * NA_KW - 1)[:, None, None]) & col_ok[None]).astype(np.float32)
    toep = jnp.einsum("hrd,dck->hcrk", rpb_l, onehot, precision=lax.Precision.HIGHEST)
    toep = jnp.where(col_ok[None, :, None, :], toep, NEG_BIG)
    pad = NA_BAND
    toep = jnp.pad(toep, ((0, 0), (0, 0), (pad, pad), (0, 0)), constant_values=NEG_BIG)
    toep = toep.reshape(NA_HEADS, GRID_W, -1)
    cases = []
    for rt in tiles:
        rs = rt * NA_Q_ROWS
        b0 = int(np.clip(rs - NA_KH // 2, 0, rows - NA_BAND))
        krow = b0 + np.arange(NA_BAND)
        per_row = []
        for i in range(NA_Q_ROWS):
            qrow = rs + i
            r0 = int(np.clip(qrow - NA_KH // 2, 0, rows - NA_KH))
            ok = np.repeat((krow >= r0) & (krow < r0 + NA_KH), GRID_W)
            start = (b0 - qrow + (NA_KH - 1) + pad) * GRID_W
            blk = toep[:, :, start:start + NA_BAND * GRID_W]
            per_row.append(jnp.where(ok[None, None, :], blk, NEG_BIG))
        cases.append(jnp.stack(per_row, axis=1))
    tab = jnp.stack(cases, axis=0)
    return tab.reshape(len(tiles), NA_HEADS // 2, 2 * NA_Q_ROWS * GRID_W, NA_BAND * GRID_W)


def _na_kernel(q_ref, k_ref, v_ref, bias_ref, o_ref, *, rows):
    tq, tk = NA_Q_ROWS * GRID_W, NA_BAND * GRID_W
    n_sub = NA_STEP_ROWS // NA_Q_ROWS
    n_edge = _na_edge_tiles()
    n_tiles = rows // NA_Q_ROWS
    t0 = pl.program_id(2) * n_sub
    lane = lax.broadcasted_iota(jnp.int32, (1, LANES), 1)
    first_head = lane < NA_HEAD_DIM

    for t in range(n_sub):
        rt = t0 + t
        b0 = jnp.clip(rt * NA_Q_ROWS - NA_KH // 2, 0, rows - NA_BAND)
        koff = pl.multiple_of(b0 * GRID_W, GRID_W)
        kb = k_ref[pl.ds(koff, tk), :]
        vb = v_ref[pl.ds(koff, tk), :]
        q = q_ref[t * tq:(t + 1) * tq, :]
        case = jnp.where(rt < n_edge, rt,
                         jnp.where(rt >= n_tiles - n_edge, rt - (n_tiles - n_edge) + n_edge + 1, n_edge))
        zero = jnp.zeros_like(q)
        q2 = jnp.concatenate([jnp.where(first_head, q, zero), jnp.where(first_head, zero, q)], axis=0)
        q2 = q2 * (NA_HEAD_DIM ** -0.5)
        s = lax.dot_general(q2, kb, (((1,), (1,)), ((), ())), preferred_element_type=F32)
        s = s + bias_ref[case]
        m = jnp.max(s, axis=-1, keepdims=True)
        e = jnp.exp(s - m)
        denom = jnp.sum(e, axis=-1, keepdims=True)
        o2 = jnp.dot(e.astype(BF16), vb, preferred_element_type=F32) * (1.0 / denom)
        o_ref[t * tq:(t + 1) * tq, :] = jnp.where(first_head, o2[:tq], o2[tq:]).astype(BF16)


def _natten(qkv, bias_tab, batch, seq):
    n = qkv.shape[0]
    rows = seq // GRID_W
    assert rows % NA_STEP_ROWS == 0 and rows >= 2 * NA_KH + NA_Q_ROWS * (2 * _na_edge_tiles() + 1)
    d_na = NA_HEADS * NA_HEAD_DIM
    n_pairs = d_na // LANES
    steps = rows // NA_STEP_ROWS
    tqs = NA_STEP_ROWS * GRID_W
    tq, tk = NA_Q_ROWS * GRID_W, NA_BAND * GRID_W
    n_case = bias_tab.shape[0]
    vmem = 2 * (2 * tqs * LANES * 2 + 2 * seq * LANES * 2 + n_case * 2 * tq * tk * 4) + 24 * tq * tk * 4
    return pl.pallas_call(
        functools.partial(_na_kernel, rows=rows),
        grid=(batch, n_pairs, steps),
        in_specs=[
            pl.BlockSpec((tqs, LANES), lambda b, hp, r: (b * steps + r, hp)),
            pl.BlockSpec((seq, LANES), lambda b, hp, r: (b, n_pairs + hp)),
            pl.BlockSpec((seq, LANES), lambda b, hp, r: (b, 2 * n_pairs + hp)),
            pl.BlockSpec((n_case, None, 2 * tq, tk), lambda b, hp, r: (0, hp, 0, 0)),
        ],
        out_specs=pl.BlockSpec((tqs, LANES), lambda b, hp, r: (b * steps + r, hp)),
        out_shape=jax.ShapeDtypeStruct((n, d_na), BF16),
        compiler_params=_params(("parallel", "parallel", "arbitrary"), vmem),
        name="natten",
    )(qkv, qkv, qkv, bias_tab)


def _merge_kernel(x_ref, pa_ref, pb_ref, pc_ref, wg0_ref, wg1_ref, wg2_ref, wa_ref, wb_ref, wc_ref, o_ref):
    x = x_ref[...]
    acc = None
    for br_ref, wg_ref, wbr_ref in ((pa_ref, wg0_ref, wa_ref), (pb_ref, wg1_ref, wb_ref), (pc_ref, wg2_ref, wc_ref)):
        gate = jax.nn.sigmoid(jnp.dot(x, wg_ref[...], preferred_element_type=F32))
        y = jnp.dot(br_ref[...], wbr_ref[...], preferred_element_type=F32)
        acc = gate * y if acc is None else acc + gate * y
    o_ref[...] = acc.astype(BF16)


def _merge(xb, pa, pb, pc, w_in_b, w_a, w_b, w_c, layer):
    n, d = xb.shape
    dbr = pa.shape[1]
    tm, tn = MERGE_TM, MERGE_TN
    gate0 = (w_in_b.shape[2] - N_BRANCH * d) // tn
    per = d // tn
    vmem = 2 * (tm * d * 2 + 3 * tm * dbr * 2 + 3 * d * tn * 2 + 3 * dbr * tn * 2 + tm * tn * 2) + 4 * tm * tn * 4
    gate_spec = lambda b: pl.BlockSpec((None, d, tn), lambda i, j: (layer, 0, gate0 + b * per + j))
    br_spec = pl.BlockSpec((tm, dbr), lambda i, j: (i, 0))
    wbr_spec = pl.BlockSpec((None, dbr, tn), lambda i, j: (layer, 0, j))
    return pl.pallas_call(
        _merge_kernel,
        grid=(n // tm, per),
        in_specs=[pl.BlockSpec((tm, d), lambda i, j: (i, 0)), br_spec, br_spec, br_spec,
                  gate_spec(0), gate_spec(1), gate_spec(2), wbr_spec, wbr_spec, wbr_spec],
        out_specs=pl.BlockSpec((tm, tn), lambda i, j: (i, j)),
        out_shape=jax.ShapeDtypeStruct((n, d), BF16),
        compiler_params=_params(("parallel", "arbitrary"), vmem),
        name="merge",
    )(xb, pa, pb, pc, w_in_b, w_in_b, w_in_b, w_a, w_b, w_c)


def _out_ln_kernel(m_ref, x_ref, w_ref, g_ref, b_ref, of_ref, ob_ref, *, alpha):
    for r in range(m_ref.shape[0] // OUT_ROWS):
        rows = slice(r * OUT_ROWS, (r + 1) * OUT_ROWS)
        y = alpha * x_ref[rows, :] + jnp.dot(m_ref[rows, :], w_ref[...], preferred_element_type=F32)
        y = _layer_norm(y, g_ref[...], b_ref[...])
        of_ref[rows, :] = y
        ob_ref[rows, :] = y.astype(BF16)


def _out_ln(merged, xf, w_out_b, g, b, layer, alpha):
    n, d = xf.shape
    tm = OUT_TM
    vmem = 2 * (tm * d * 2 + tm * d * 4 + d * d * 2 + tm * d * 4 + tm * d * 2) + 4 * OUT_ROWS * d * 4
    row = pl.BlockSpec((tm, d), lambda i: (i, 0))
    vec = pl.BlockSpec((None, 1, d), lambda i: (layer, 0, 0))
    return pl.pallas_call(
        functools.partial(_out_ln_kernel, alpha=alpha),
        grid=(n // tm,),
        in_specs=[row, row, pl.BlockSpec((None, d, d), lambda i: (layer, 0, 0)), vec, vec],
        out_specs=[row, row],
        out_shape=[jax.ShapeDtypeStruct((n, d), F32), jax.ShapeDtypeStruct((n, d), BF16)],
        compiler_params=_params(("parallel",), vmem),
        name="out_ln",
    )(merged, xf, w_out_b, g, b)


def _ffn_kernel(xb_ref, xf_ref, wu_ref, wd_ref, g_ref, b_ref, of_ref, *, alpha):
    f = pl.program_id(1)

    @pl.when(f == 0)
    def _():
        of_ref[...] = alpha * xf_ref[...]

    h = jnp.maximum(jnp.dot(xb_ref[...], wu_ref[...], preferred_element_type=F32), 0.0)
    of_ref[...] += jnp.dot((h * h).astype(BF16), wd_ref[...], preferred_element_type=F32)

    @pl.when(f == pl.num_programs(1) - 1)
    def _():
        of_ref[...] = _layer_norm(of_ref[...], g_ref[...], b_ref[...])


def _ffn(xb, xf, w_up_b, w_down_b, g, b, layer, alpha):
    n, d = xf.shape
    dff = w_up_b.shape[2]
    tm, tf = FFN_TM, FFN_TF
    vmem = 2 * (tm * d * 2 + tm * d * 4 + d * tf * 2 + tf * d * 2 + tm * d * 4) + 3 * tm * tf * 4
    row = pl.BlockSpec((tm, d), lambda i, f: (i, 0))
    vec = pl.BlockSpec((None, 1, d), lambda i, f: (layer, 0, 0))
    return pl.pallas_call(
        functools.partial(_ffn_kernel, alpha=alpha),
        grid=(n // tm, dff // tf),
        in_specs=[row, row, pl.BlockSpec((None, d, tf), lambda i, f: (layer, 0, f)),
                  pl.BlockSpec((None, tf, d), lambda i, f: (layer, f, 0)), vec, vec],
        out_specs=row,
        out_shape=jax.ShapeDtypeStruct((n, d), F32),
        compiler_params=_params(("parallel", "arbitrary"), vmem),
        name="ffn",
    )(xb, xf, w_up_b, w_down_b, g, b)


def kernel(x_prompt, x_sample, w_in, w_pool, s_pool, sgu_ln_g, sgu_ln_b, w_s, b_s, rpb, w_br_pool, w_br_sgu,
           w_br_na, w_out, ln1_g, ln1_b, w_up, w_down, ln2_g, ln2_b):
    depth = w_in.shape[0]
    alpha = float((2 * depth) ** 0.25)
    w_in_b, w_pool_b, w_s_b = w_in.astype(BF16), w_pool.astype(BF16), w_s.astype(BF16)
    w_a, w_b, w_c = w_br_pool.astype(BF16), w_br_sgu.astype(BF16), w_br_na.astype(BF16)
    w_out_b, w_up_b, w_down_b = w_out.astype(BF16), w_up.astype(BF16), w_down.astype(BF16)
    vec = lambda v: v[:, None, :]
    sgu_g, sgu_b, s_pool_v, b_s_col = vec(sgu_ln_g), vec(sgu_ln_b), vec(s_pool), b_s[..., None]
    ln1 = (vec(ln1_g), vec(ln1_b))
    ln2 = (vec(ln2_g), vec(ln2_b))
    bias_tabs = [_na_bias_table(rpb[l]) for l in range(depth)]

    outs = []
    for x in (x_prompt, x_sample):
        batch, seq, d = x.shape
        xf = x.reshape(batch * seq, d)
        for l in range(depth):
            a, xb = _proj_cast(xf, w_in_b, l)
            ug =_proj(xb, w_in_b, l, 1, 1, "gelu", BF16, sgu_g, sgu_b)
            vn = _proj(xb, w_in_b, l, 2, 1, "gelu_ln", BF16, sgu_g, sgu_b)
            qkv = _proj(xb, w_in_b, l, 3, 3, "plain", BF16, sgu_g, sgu_b)
            pa = _pool(a, w_pool_b, s_pool_v, l, seq)
            pb = _sgu(ug, vn, w_s_b, b_s_col, l)
            pc = _natten(qkv, bias_tabs[l], batch, seq)
            merged = _merge(xb, pa, pb, pc, w_in_b, w_a, w_b, w_c, l)
            xf, xb = _out_ln(merged, xf, w_out_b, *ln1, l, alpha)
            xf = _ffn(xb, xf, w_up_b, w_down_b, *ln2, l, alpha)
        outs.append(xf.reshape(batch, seq, d))
    return tuple(outs)
```

```python
import functools

import jax
import jax.numpy as jnp
import numpy as np
from jax import lax
from jax.experimental import pallas as pl
from jax.experimental.pallas import tpu as pltpu

F32 = jnp.float32
BF16 = jnp.bfloat16

GRID_W = 64
POOL_WINDOWS = (2, 4, 8, 16)
POOL_HALO = max(POOL_WINDOWS) // 2
CHUNK = 128
SGU_GROUPS = 8
NA_HEADS = 16
NA_HEAD_DIM = 64
NA_KH = 8
NA_KW = 16
N_BRANCH = 3
LN_EPS = 1e-5
NEG_BIG = float("-inf")

LANES = 128
V7X_VMEM_BYTES = 64 * 1024 * 1024
VMEM_COMPILER_RESERVE = 6 * 1024 * 1024
VMEM_HEADROOM = 3 * 1024 * 1024

PROJ_TM = 2048
PROJ_TN = 1024
PROJ_ROWS = 256
POOL_TM = 512
SGU_TM = 2048
NA_Q_ROWS = 4
NA_BAND = NA_Q_ROWS + NA_KH - 1
NA_STEP_ROWS = 64
MERGE_TM = 1024
MERGE_TN = 512
OUT_TM = 512
OUT_ROWS = 128
FFN_TM = 512
FFN_TF = 1024
CAST_TM = 1024


def _params(semantics, vmem_estimate):
    limit = min(int(vmem_estimate) + VMEM_COMPILER_RESERVE, V7X_VMEM_BYTES - VMEM_HEADROOM)
    return pltpu.CompilerParams(dimension_semantics=semantics, vmem_limit_bytes=limit)


def _layer_norm(y, g, b):
    mu = jnp.mean(y, axis=-1, keepdims=True)
    d = y - mu
    var = jnp.mean(d * d, axis=-1, keepdims=True)
    return d * lax.rsqrt(var + LN_EPS) * g + b


def _proj_kernel(x_ref, w_ref, g_ref, b_ref, o_ref, *, mode):
    for r in range(x_ref.shape[0] // PROJ_ROWS):
        rows = slice(r * PROJ_ROWS, (r + 1) * PROJ_ROWS)
        acc = jnp.dot(x_ref[rows, :], w_ref[...], preferred_element_type=F32)
        if mode == "gelu":
            acc = jax.nn.gelu(acc)
        elif mode == "gelu_ln":
            acc = _layer_norm(jax.nn.gelu(acc), g_ref[...], b_ref[...])
        o_ref[rows, :] = acc.astype(o_ref.dtype)


def _proj(xb, w_in_b, layer, seg0, nseg, mode, out_dtype, ln_g, ln_b):
    n, d = xb.shape
    tm, tn = PROJ_TM, PROJ_TN
    osz = jnp.dtype(out_dtype).itemsize
    vmem = 2 * (tm * d * 2 + d * tn * 2 + tm * tn * osz) + 4 * PROJ_ROWS * tn * 4
    return pl.pallas_call(
        functools.partial(_proj_kernel, mode=mode),
        grid=(n // tm, nseg),
        in_specs=[
            pl.BlockSpec((tm, d), lambda i, j: (i, 0)),
            pl.BlockSpec((None, d, tn), lambda i, j: (layer, 0, seg0 + j)),
            pl.BlockSpec((None, 1, tn), lambda i, j: (layer, 0, 0)),
            pl.BlockSpec((None, 1, tn), lambda i, j: (layer, 0, 0)),
        ],
        out_specs=pl.BlockSpec((tm, tn), lambda i, j: (i, j)),
        out_shape=jax.ShapeDtypeStruct((n, nseg * tn), out_dtype),
        compiler_params=_params(("parallel", "arbitrary"), vmem),
        name="proj%d_%s" % (seg0, mode),
    )(xb, w_in_b, ln_g, ln_b)


def _proj_cast_kernel(x_ref, w_ref, a_ref, xb_ref):
    for r in range(x_ref.shape[0] // PROJ_ROWS):
        rows = slice(r * PROJ_ROWS, (r + 1) * PROJ_ROWS)
        xb = x_ref[rows, :].astype(BF16)
        xb_ref[rows, :] = xb
        a_ref[rows, :] = jnp.dot(xb, w_ref[...], preferred_element_type=F32)


def _proj_cast(xf, w_in_b, layer):
    n, d = xf.shape
    tm, tn = CAST_TM, PROJ_TN
    vmem = 2 * (tm * d * 4 + d * tn * 2 + tm * tn * 4 + tm * d * 2) + 4 * PROJ_ROWS * tn * 4
    return pl.pallas_call(
        _proj_cast_kernel,
        grid=(n // tm,),
        in_specs=[pl.BlockSpec((tm, d), lambda i: (i, 0)),
                  pl.BlockSpec((None, d, tn), lambda i: (layer, 0, 0))],
        out_specs=[pl.BlockSpec((tm, tn), lambda i: (i, 0)), pl.BlockSpec((tm, d), lambda i: (i, 0))],
        out_shape=[jax.ShapeDtypeStruct((n, tn), F32), jax.ShapeDtypeStruct((n, d), BF16)],
        compiler_params=_params(("parallel",), vmem),
        name="proj0_cast",
    )(xf, w_in_b)


def _pool_kernel(ap_ref, ac_ref, an_ref, wp_ref, sp_ref, o_ref, buf_ref, *, seq, tm):
    i = pl.program_id(0)
    s0 = (i * tm) % seq
    h = POOL_HALO
    buf_ref[0:h, :] = jnp.where(s0 == 0, 0.0, ap_ref[...])
    buf_ref[h:h + tm, :] = ac_ref[...]
    buf_ref[h + tm:2 * h + tm, :] = jnp.where(s0 + tm == seq, 0.0, an_ref[...])
    pos = s0 + lax.broadcasted_iota(jnp.int32, (tm, 1), 0)
    gc = ac_ref.shape[1] // len(POOL_WINDOWS)
    for gi, w in enumerate(POOL_WINDOWS):
        cols = slice(gi * gc, (gi + 1) * gc)
        tot = buf_ref[h - w // 2:h - w // 2 + tm, cols]
        for dlt in range(-w // 2 + 1, w // 2):
            tot = tot + buf_ref[h + dlt:h + dlt + tm, cols]
        cnt = jnp.minimum(pos + w // 2, seq) - jnp.maximum(pos - w // 2, 0)
        p = tot * (1.0 / cnt.astype(F32)) - ac_ref[:, cols]
        y = jnp.dot(p.astype(BF16), wp_ref[gi], preferred_element_type=F32)
        o_ref[:, cols] = (y * sp_ref[:, cols]).astype(BF16)


def _pool(a, w_pool_b, s_pool, layer, seq):
    n, dp = a.shape
    tm, h = POOL_TM, POOL_HALO
    assert seq % tm == 0 and tm % h == 0
    nb = tm // h
    last = n // h - 1
    vmem = 2 * (tm * dp * 4 + 2 * h * dp * 4 + tm * dp * 2) + 4 * dp * dp + (tm + 2 * h) * dp * 4
    return pl.pallas_call(
        functools.partial(_pool_kernel, seq=seq, tm=tm),
        grid=(n // tm,),
        in_specs=[
            pl.BlockSpec((h, dp), lambda i: (jnp.maximum(i * nb - 1, 0), 0)),
            pl.BlockSpec((tm, dp), lambda i: (i, 0)),
            pl.BlockSpec((h, dp), lambda i: (jnp.minimum((i + 1) * nb, last), 0)),
            pl.BlockSpec((None,) + w_pool_b.shape[1:], lambda i: (layer, 0, 0, 0)),
            pl.BlockSpec((None, 1, dp), lambda i: (layer, 0, 0)),
        ],
        out_specs=pl.BlockSpec((tm, dp), lambda i: (i, 0)),
        out_shape=jax.ShapeDtypeStruct((n, dp), BF16),
        scratch_shapes=[pltpu.VMEM((tm + 2 * h, dp), F32)],
        compiler_params=_params(("parallel",), vmem),
        name="pool",
    )(a, a, a, w_pool_b, s_pool)


def _sgu_kernel(u_ref, v_ref, ws_ref, bs_ref, o_ref):
    tm, ds = u_ref.shape
    gc = ds // SGU_GROUPS
    n_chunks = tm // CHUNK
    for g in range(SGU_GROUPS):
        cols = slice(g * gc, (g + 1) * gc)
        v_all = jnp.concatenate([v_ref[c * CHUNK:(c + 1) * CHUNK, cols] for c in range(n_chunks)], axis=1)
        sg = jnp.dot(ws_ref[g], v_all, preferred_element_type=F32) + bs_ref[g]
        for c in range(n_chunks):
            rows = slice(c * CHUNK, (c + 1) * CHUNK)
            o_ref[rows, cols] = (u_ref[rows, cols].astype(F32) * sg[:, c * gc:(c + 1) * gc]).astype(BF16)


def _sgu(ug, vn, w_s_b, b_s_col, layer):
    n, ds = ug.shape
    tm = SGU_TM
    vmem = 2 * (3 * tm * ds * 2) + 2 * SGU_GROUPS * CHUNK * (CHUNK * 2 + LANES * 4)
    row = pl.BlockSpec((tm, ds), lambda i: (i, 0))
    return pl.pallas_call(
        _sgu_kernel,
        grid=(n // tm,),
        in_specs=[row, row,
                  pl.BlockSpec((None,) + w_s_b.shape[1:], lambda i: (layer, 0, 0, 0)),
                  pl.BlockSpec((None,) + b_s_col.shape[1:], lambda i: (layer, 0, 0, 0))],
        out_specs=row,
        out_shape=jax.ShapeDtypeStruct((n, ds), BF16),
        compiler_params=_params(("parallel",), vmem),
        name="sgu",
    )(ug, vn, w_s_b, b_s_col)


def _na_edge_tiles():
    return -(-(NA_KH // 2) // NA_Q_ROWS)


def _na_bias_table(rpb_l):
    n_edge = _na_edge_tiles()
    rows = 2 * NA_KH + NA_Q_ROWS * (2 * n_edge + 1)
    n_tiles = rows // NA_Q_ROWS
    tiles = list(range(n_edge)) + [n_edge] + list(range(n_tiles - n_edge, n_tiles))
    c = np.arange(GRID_W)
    c0 = np.clip(c - NA_KW // 2, 0, GRID_W - NA_KW)
    col_ok = (c[None, :] >= c0[:, None]) & (c[None, :] < c0[:, None] + NA_KW)
    dc = c[None, :] - c[:, None] + (NA_KW - 1)
    onehot = ((dc[None] == np.arange(2 * NA_KW - 1)[:, None, None]) & col_ok[None]).astype(np.float32)
    toep = jnp.einsum("hrd,dck->hcrk", rpb_l, onehot, precision=lax.Precision.HIGHEST)
    toep = jnp.where(col_ok[None, :, None, :], toep, NEG_BIG)
    pad = NA_BAND
    toep = jnp.pad(toep, ((0, 0), (0, 0), (pad, pad), (0, 0)), constant_values=NEG_BIG)
    toep = toep.reshape(NA_HEADS, GRID_W, -1)
    cases = []
    for rt in tiles:
        rs = rt * NA_Q_ROWS
        b0 = int(np.clip(rs - NA_KH // 2, 0, rows - NA_BAND))
        krow = b0 + np.arange(NA_BAND)
        per_row = []
        for i in range(NA_Q_ROWS):
            qrow = rs + i
            r0 = int(np.clip(qrow - NA_KH // 2, 0, rows - NA_KH))
            ok = np.repeat((krow >= r0) & (krow < r0 + NA_KH), GRID_W)
            start = (b0 - qrow + (NA_KH - 1) + pad) * GRID_W
            blk = toep[:, :, start:start + NA_BAND * GRID_W]
            per_row.append(jnp.where(ok[None, None, :], blk, NEG_BIG))
        cases.append(jnp.stack(per_row, axis=1))
    tab = jnp.stack(cases, axis=0)
    return tab.reshape(len(tiles), NA_HEADS // 2, 2 * NA_Q_ROWS * GRID_W, NA_BAND * GRID_W)


def _na_kernel(q_ref, k_ref, v_ref, bias_ref, o_ref, *, rows):
    tq, tk = NA_Q_ROWS * GRID_W, NA_BAND * GRID_W
    n_sub = NA_STEP_ROWS // NA_Q_ROWS
    n_edge = _na_edge_tiles()
    n_tiles = rows // NA_Q_ROWS
    t0 = pl.program_id(2) * n_sub
    lane = lax.broadcasted_iota(jnp.int32, (1, LANES), 1)
    first_head = lane < NA_HEAD_DIM

    for t in range(n_sub):
        rt = t0 + t
        b0 = jnp.clip(rt * NA_Q_ROWS - NA_KH // 2, 0, rows - NA_BAND)
        koff = pl.multiple_of(b0 * GRID_W, GRID_W)
        kb = k_ref[pl.ds(koff, tk), :]
        vb = v_ref[pl.ds(koff, tk), :]
        q = q_ref[t * tq:(t + 1) * tq, :]
        case = jnp.where(rt < n_edge, rt,
                         jnp.where(rt >= n_tiles - n_edge, rt - (n_tiles - n_edge) + n_edge + 1, n_edge))
        zero = jnp.zeros_like(q)
        q2 = jnp.concatenate([jnp.where(first_head, q, zero), jnp.where(first_head, zero, q)], axis=0)
        q2 = q2 * (NA_HEAD_DIM ** -0.5)
        s = lax.dot_general(q2, kb, (((1,), (1,)), ((), ())), preferred_element_type=F32)
        s = s + bias_ref[case]
        m = jnp.max(s, axis=-1, keepdims=True)
        e = jnp.exp(s - m)
        denom = jnp.sum(e, axis=-1, keepdims=True)
        o2 = jnp.dot(e.astype(BF16), vb, preferred_element_type=F32) * (1.0 / denom)
        o_ref[t * tq:(t + 1) * tq, :] = jnp.where(first_head, o2[:tq], o2[tq:]).astype(BF16)


def _natten(qkv, bias_tab, batch, seq):
    n = qkv.shape[0]
    rows = seq // GRID_W
    assert rows % NA_STEP_ROWS == 0 and rows >= 2 * NA_KH + NA_Q_ROWS * (2 * _na_edge_tiles() + 1)
    d_na = NA_HEADS * NA_HEAD_DIM
    n_pairs = d_na // LANES
    steps = rows // NA_STEP_ROWS
    tqs = NA_STEP_ROWS * GRID_W
    tq, tk = NA_Q_ROWS * GRID_W, NA_BAND * GRID_W
    n_case = bias_tab.shape[0]
    vmem = 2 * (2 * tqs * LANES * 2 + 2 * seq * LANES * 2 + n_case * 2 * tq * tk * 4) + 24 * tq * tk * 4
    return pl.pallas_call(
        functools.partial(_na_kernel, rows=rows),
        grid=(batch, n_pairs, steps),
        in_specs=[
            pl.BlockSpec((tqs, LANES), lambda b, hp, r: (b * steps + r, hp)),
            pl.BlockSpec((seq, LANES), lambda b, hp, r: (b, n_pairs + hp)),
            pl.BlockSpec((seq, LANES), lambda b, hp, r: (b, 2 * n_pairs + hp)),
            pl.BlockSpec((n_case, None, 2 * tq, tk), lambda b, hp, r: (0, hp, 0, 0)),
        ],
        out_specs=pl.BlockSpec((tqs, LANES), lambda b, hp, r: (b * steps + r, hp)),
        out_shape=jax.ShapeDtypeStruct((n, d_na), BF16),
        compiler_params=_params(("parallel", "parallel", "arbitrary"), vmem),
        name="natten",
    )(qkv, qkv, qkv, bias_tab)


def _merge_kernel(x_ref, pa_ref, pb_ref, pc_ref, wg0_ref, wg1_ref, wg2_ref, wa_ref, wb_ref, wc_ref, o_ref):
    x = x_ref[...]
    acc = None
    for br_ref, wg_ref, wbr_ref in ((pa_ref, wg0_ref, wa_ref), (pb_ref, wg1_ref, wb_ref), (pc_ref, wg2_ref, wc_ref)):
        gate = jax.nn.sigmoid(jnp.dot(x, wg_ref[...], preferred_element_type=F32))
        y = jnp.dot(br_ref[...], wbr_ref[...], preferred_element_type=F32)
        acc = gate * y if acc is None else acc + gate * y
    o_ref[...] = acc.astype(BF16)


def _merge(xb, pa, pb, pc, w_in_b, w_a, w_b, w_c, layer):
    n, d = xb.shape
    dbr = pa.shape[1]
    tm, tn = MERGE_TM, MERGE_TN
    gate0 = (w_in_b.shape[2] - N_BRANCH * d) // tn
    per = d // tn
    vmem = 2 * (tm * d * 2 + 3 * tm * dbr * 2 + 3 * d * tn * 2 + 3 * dbr * tn * 2 + tm * tn * 2) + 4 * tm * tn * 4
    gate_spec = lambda b: pl.BlockSpec((None, d, tn), lambda i, j: (layer, 0, gate0 + b * per + j))
    br_spec = pl.BlockSpec((tm, dbr), lambda i, j: (i, 0))
    wbr_spec = pl.BlockSpec((None, dbr, tn), lambda i, j: (layer, 0, j))
    return pl.pallas_call(
        _merge_kernel,
        grid=(n // tm, per),
        in_specs=[pl.BlockSpec((tm, d), lambda i, j: (i, 0)), br_spec, br_spec, br_spec,
                  gate_spec(0), gate_spec(1), gate_spec(2), wbr_spec, wbr_spec, wbr_spec],
        out_specs=pl.BlockSpec((tm, tn), lambda i, j: (i, j)),
        out_shape=jax.ShapeDtypeStruct((n, d), BF16),
        compiler_params=_params(("parallel", "arbitrary"), vmem),
        name="merge",
    )(xb, pa, pb, pc, w_in_b, w_in_b, w_in_b, w_a, w_b, w_c)


def _out_ln_kernel(m_ref, x_ref, w_ref, g_ref, b_ref, of_ref, ob_ref, *, alpha):
    for r in range(m_ref.shape[0] // OUT_ROWS):
        rows = slice(r * OUT_ROWS, (r + 1) * OUT_ROWS)
        y = alpha * x_ref[rows, :] + jnp.dot(m_ref[rows, :], w_ref[...], preferred_element_type=F32)
        y = _layer_norm(y, g_ref[...], b_ref[...])
        of_ref[rows, :] = y
        ob_ref[rows, :] = y.astype(BF16)


def _out_ln(merged, xf, w_out_b, g, b, layer, alpha):
    n, d = xf.shape
    tm = OUT_TM
    vmem = 2 * (tm * d * 2 + tm * d * 4 + d * d * 2 + tm * d * 4 + tm * d * 2) + 4 * OUT_ROWS * d * 4
    row = pl.BlockSpec((tm, d), lambda i: (i, 0))
    vec = pl.BlockSpec((None, 1, d), lambda i: (layer, 0, 0))
    return pl.pallas_call(
        functools.partial(_out_ln_kernel, alpha=alpha),
        grid=(n // tm,),
        in_specs=[row, row, pl.BlockSpec((None, d, d), lambda i: (layer, 0, 0)), vec, vec],
        out_specs=[row, row],
        out_shape=[jax.ShapeDtypeStruct((n, d), F32), jax.ShapeDtypeStruct((n, d), BF16)],
        compiler_params=_params(("parallel",), vmem),
        name="out_ln",
    )(merged, xf, w_out_b, g, b)


def _ffn_kernel(xb_ref, xf_ref, wu_ref, wd_ref, g_ref, b_ref, of_ref, *, alpha):
    f = pl.program_id(1)

    @pl.when(f == 0)
    def _():
        of_ref[...] = alpha * xf_ref[...]

    h = jnp.maximum(jnp.dot(xb_ref[...], wu_ref[...], preferred_element_type=F32), 0.0)
    of_ref[...] += jnp.dot((h * h).astype(BF16), wd_ref[...], preferred_element_type=F32)

    @pl.when(f == pl.num_programs(1) - 1)
    def _():
        of_ref[...] = _layer_norm(of_ref[...], g_ref[...], b_ref[...])


def _ffn(xb, xf, w_up_b, w_down_b, g, b, layer, alpha):
    n, d = xf.shape
    dff = w_up_b.shape[2]
    tm, tf = FFN_TM, FFN_TF
    vmem = 2 * (tm * d * 2 + tm * d * 4 + d * tf * 2 + tf * d * 2 + tm * d * 4) + 3 * tm * tf * 4
    row = pl.BlockSpec((tm, d), lambda i, f: (i, 0))
    vec = pl.BlockSpec((None, 1, d), lambda i, f: (layer, 0, 0))
    return pl.pallas_call(
        functools.partial(_ffn_kernel, alpha=alpha),
        grid=(n // tm, dff // tf),
        in_specs=[row, row, pl.BlockSpec((None, d, tf), lambda i, f: (layer, 0, f)),
                  pl.BlockSpec((None, tf, d), lambda i, f: (layer, f, 0)), vec, vec],
        out_specs=row,
        out_shape=jax.ShapeDtypeStruct((n, d), F32),
        compiler_params=_params(("parallel", "arbitrary"), vmem),
        name="ffn",
    )(xb, xf, w_up_b, w_down_b, g, b)


def kernel(x_prompt, x_sample, w_in, w_pool, s_pool, sgu_ln_g, sgu_ln_b, w_s, b_s, rpb, w_br_pool, w_br_sgu,
           w_br_na, w_out, ln1_g, ln1_b, w_up, w_down, ln2_g, ln2_b):
    depth = w_in.shape[0]
    alpha = float((2 * depth) ** 0.25)
    w_in_b, w_pool_b, w_s_b = w_in.astype(BF16), w_pool.astype(BF16), w_s.astype(BF16)
    w_a, w_b, w_c = w_br_pool.astype(BF16), w_br_sgu.astype(BF16), w_br_na.astype(BF16)
    w_out_b, w_up_b, w_down_b = w_out.astype(BF16), w_up.astype(BF16), w_down.astype(BF16)
    vec = lambda v: v[:, None, :]
    sgu_g, sgu_b, s_pool_v, b_s_col = vec(sgu_ln_g), vec(sgu_ln_b), vec(s_pool), b_s[..., None]
    ln1 = (vec(ln1_g), vec(ln1_b))
    ln2 = (vec(ln2_g), vec(ln2_b))
    bias_tabs = [_na_bias_table(rpb[l]) for l in range(depth)]

    outs = []
    for x in (x_prompt, x_sample):
        batch, seq, d = x.shape
        xf = x.reshape(batch * seq, d)
        for l in range(depth):
            a, xb = _proj_cast(xf, w_in_b, l)
            ug =_proj(xb, w_in_b, l, 1, 1, "gelu", BF16, sgu_g, sgu_b)
            vn = _proj(xb, w_in_b, l, 2, 1, "gelu_ln", BF16, sgu_g, sgu_b)
            qkv = _proj(xb, w_in_b, l, 3, 3, "plain", BF16, sgu_g, sgu_b)
            pa = _pool(a, w_pool_b, s_pool_v, l, seq)
            pb = _sgu(ug, vn, w_s_b, b_s_col, l)
            pc = _natten(qkv, bias_tabs[l], batch, seq)
            merged = _merge(xb, pa, pb, pc, w_in_b, w_a, w_b, w_c, l)
            xf, xb = _out_ln(merged, xf, w_out_b, *ln1, l, alpha)
            xf = _ffn(xb, xf, w_up_b, w_down_b, *ln2, l, alpha)
        outs.append(xf.reshape(batch, seq, d))
    return tuple(outs)
```

```python
import functools

import jax
import jax.numpy as jnp
import numpy as np
from jax import lax
from jax.experimental import pallas as pl
from jax.experimental.pallas import tpu as pltpu

F32 = jnp.float32
BF16 = jnp.bfloat16

GRID_W = 64
POOL_WINDOWS = (2, 4, 8, 16)
POOL_HALO = max(POOL_WINDOWS) // 2
CHUNK = 128
SGU_GROUPS = 8
NA_HEADS = 16
NA_HEAD_DIM = 64
NA_KH = 8
NA_KW = 16
N_BRANCH = 3
LN_EPS = 1e-5
NEG_BIG = float("-inf")

LANES = 128
V7X_VMEM_BYTES = 64 * 1024 * 1024
VMEM_COMPILER_RESERVE = 6 * 1024 * 1024
VMEM_HEADROOM = 3 * 1024 * 1024

PROJ_TM = 2048
PROJ_TN = 1024
PROJ_ROWS = 256
POOL_TM = 512
NA_Q_ROWS = 4
NA_BAND = NA_Q_ROWS + NA_KH - 1
NA_STEP_ROWS = 64
MERGE_TM = 1024
MERGE_TN = 512
OUT_TM = 512
OUT_ROWS = 128
FFN_TM = 512
FFN_TF = 1024
CAST_TM = 1024
UV_TM = 1024


def _params(semantics, vmem_estimate):
    limit = min(int(vmem_estimate) + VMEM_COMPILER_RESERVE, V7X_VMEM_BYTES - VMEM_HEADROOM)
    return pltpu.CompilerParams(dimension_semantics=semantics, vmem_limit_bytes=limit)


def _layer_norm(y, g, b):
    mu = jnp.mean(y, axis=-1, keepdims=True)
    d = y - mu
    var = jnp.mean(d * d, axis=-1, keepdims=True)
    return d * lax.rsqrt(var + LN_EPS) * g + b


def _proj_kernel(x_ref, w_ref, g_ref, b_ref, o_ref, *, mode):
    for r in range(x_ref.shape[0] // PROJ_ROWS):
        rows = slice(r * PROJ_ROWS, (r + 1) * PROJ_ROWS)
        acc = jnp.dot(x_ref[rows, :], w_ref[...], preferred_element_type=F32)
        if mode == "gelu":
            acc = jax.nn.gelu(acc)
        elif mode == "gelu_ln":
            acc = _layer_norm(jax.nn.gelu(acc), g_ref[...], b_ref[...])
        o_ref[rows, :] = acc.astype(o_ref.dtype)


def _proj(xb, w_in_b, layer, seg0, nseg, mode, out_dtype, ln_g, ln_b):
    n, d = xb.shape
    tm, tn = PROJ_TM, PROJ_TN
    osz = jnp.dtype(out_dtype).itemsize
    vmem = 2 * (tm * d * 2 + d * tn * 2 + tm * tn * osz) + 4 * PROJ_ROWS * tn * 4
    return pl.pallas_call(
        functools.partial(_proj_kernel, mode=mode),
        grid=(n // tm, nseg),
        in_specs=[
            pl.BlockSpec((tm, d), lambda i, j: (i, 0)),
            pl.BlockSpec((None, d, tn), lambda i, j: (layer, 0, seg0 + j)),
            pl.BlockSpec((None, 1, tn), lambda i, j: (layer, 0, 0)),
            pl.BlockSpec((None, 1, tn), lambda i, j: (layer, 0, 0)),
        ],
        out_specs=pl.BlockSpec((tm, tn), lambda i, j: (i, j)),
        out_shape=jax.ShapeDtypeStruct((n, nseg * tn), out_dtype),
        compiler_params=_params(("parallel", "arbitrary"), vmem),
        name="proj%d_%s" % (seg0, mode),
    )(xb, w_in_b, ln_g, ln_b)


def _proj_cast_kernel(x_ref, w_ref, a_ref, xb_ref):
    for r in range(x_ref.shape[0] // PROJ_ROWS):
        rows = slice(r * PROJ_ROWS, (r + 1) * PROJ_ROWS)
        xb = x_ref[rows, :].astype(BF16)
        xb_ref[rows, :] = xb
        a_ref[rows, :] = jnp.dot(xb, w_ref[...], preferred_element_type=F32)


def _proj_cast(xf, w_in_b, layer):
    n, d = xf.shape
    tm, tn = CAST_TM, PROJ_TN
    vmem = 2 * (tm * d * 4 + d * tn * 2 + tm * tn * 4 + tm * d * 2) + 4 * PROJ_ROWS * tn * 4
    return pl.pallas_call(
        _proj_cast_kernel,
        grid=(n // tm,),
        in_specs=[pl.BlockSpec((tm, d), lambda i: (i, 0)),
                  pl.BlockSpec((None, d, tn), lambda i: (layer, 0, 0))],
        out_specs=[pl.BlockSpec((tm, tn), lambda i: (i, 0)), pl.BlockSpec((tm, d), lambda i: (i, 0))],
        out_shape=[jax.ShapeDtypeStruct((n, tn), F32), jax.ShapeDtypeStruct((n, d), BF16)],
        compiler_params=_params(("parallel",), vmem),
        name="proj0_cast",
    )(xf, w_in_b)


def _pool_kernel(ap_ref, ac_ref, an_ref, wp_ref, sp_ref, o_ref, buf_ref, *, seq, tm):
    i = pl.program_id(0)
    s0 = (i * tm) % seq
    h = POOL_HALO
    buf_ref[0:h, :] = jnp.where(s0 == 0, 0.0, ap_ref[...])
    buf_ref[h:h + tm, :] = ac_ref[...]
    buf_ref[h + tm:2 * h + tm, :] = jnp.where(s0 + tm == seq, 0.0, an_ref[...])
    pos = s0 + lax.broadcasted_iota(jnp.int32, (tm, 1), 0)
    gc = ac_ref.shape[1] // len(POOL_WINDOWS)
    for gi, w in enumerate(POOL_WINDOWS):
        cols = slice(gi * gc, (gi + 1) * gc)
        tot = buf_ref[h - w // 2:h - w // 2 + tm, cols]
        for dlt in range(-w // 2 + 1, w // 2):
            tot = tot + buf_ref[h + dlt:h + dlt + tm, cols]
        cnt = jnp.minimum(pos + w // 2, seq) - jnp.maximum(pos - w // 2, 0)
        p = tot * (1.0 / cnt.astype(F32)) - ac_ref[:, cols]
        y = jnp.dot(p.astype(BF16), wp_ref[gi], preferred_element_type=F32)
        o_ref[:, cols] = (y * sp_ref[:, cols]).astype(BF16)


def _pool(a, w_pool_b, s_pool, layer, seq):
    n, dp = a.shape
    tm, h = POOL_TM, POOL_HALO
    assert seq % tm == 0 and tm % h == 0
    nb = tm // h
    last = n // h - 1
    vmem = 2 * (tm * dp * 4 + 2 * h * dp * 4 + tm * dp * 2) + 4 * dp * dp + (tm + 2 * h) * dp * 4
    return pl.pallas_call(
        functools.partial(_pool_kernel, seq=seq, tm=tm),
        grid=(n // tm,),
        in_specs=[
            pl.BlockSpec((h, dp), lambda i: (jnp.maximum(i * nb - 1, 0), 0)),
            pl.BlockSpec((tm, dp), lambda i: (i, 0)),
            pl.BlockSpec((h, dp), lambda i: (jnp.minimum((i + 1) * nb, last), 0)),
            pl.BlockSpec((None,) + w_pool_b.shape[1:], lambda i: (layer, 0, 0, 0)),
            pl.BlockSpec((None, 1, dp), lambda i: (layer, 0, 0)),
        ],
        out_specs=pl.BlockSpec((tm, dp), lambda i: (i, 0)),
        out_shape=jax.ShapeDtypeStruct((n, dp), BF16),
        scratch_shapes=[pltpu.VMEM((tm + 2 * h, dp), F32)],
        compiler_params=_params(("parallel",), vmem),
        name="pool",
    )(a, a, a, w_pool_b, s_pool)


def _uv_sgu_kernel(x_ref, wu_ref, wv_ref, g_ref, b_ref, ws_ref, bs_ref, o_ref):
    gc = o_ref.shape[1] // SGU_GROUPS
    per = PROJ_ROWS // CHUNK
    for r in range(x_ref.shape[0] // PROJ_ROWS):
        rows = slice(r * PROJ_ROWS, (r + 1) * PROJ_ROWS)
        x = x_ref[rows, :]
        u = jax.nn.gelu(jnp.dot(x, wu_ref[...], preferred_element_type=F32))
        v = _layer_norm(jax.nn.gelu(jnp.dot(x, wv_ref[...], preferred_element_type=F32)), g_ref[...], b_ref[...])
        v = v.astype(BF16)
        for g in range(SGU_GROUPS):
            cols = slice(g * gc, (g + 1) * gc)
            v_all = jnp.concatenate([v[c * CHUNK:(c + 1) * CHUNK, cols] for c in range(per)], axis=1)
            sg = jnp.dot(ws_ref[g], v_all, preferred_element_type=F32) + bs_ref[g]
            for c in range(per):
                sub = slice(c * CHUNK, (c + 1) * CHUNK)
                o_ref[r * PROJ_ROWS + c * CHUNK:r * PROJ_ROWS + (c + 1) * CHUNK, cols] = (
                    u[sub, cols] * sg[:, c * gc:(c + 1) * gc]).astype(BF16)


def _uv_sgu(xb, w_in_b, ln_g, ln_b, w_s_b, b_s_col, layer):
    n, d = xb.shape
    tm, tn = UV_TM, PROJ_TN
    vmem = 2 * (tm * d * 2 + 2 * d * tn * 2 + tm * tn * 2) + 8 * PROJ_ROWS * tn * 4
    vec = pl.BlockSpec((None, 1, tn), lambda i: (layer, 0, 0))
    return pl.pallas_call(
        _uv_sgu_kernel,
        grid=(n // tm,),
        in_specs=[pl.BlockSpec((tm, d), lambda i: (i, 0)),
                  pl.BlockSpec((None, d, tn), lambda i: (layer, 0, 1)),
                  pl.BlockSpec((None, d, tn), lambda i: (layer, 0, 2)),
                  vec, vec,
                  pl.BlockSpec((None,) + w_s_b.shape[1:], lambda i: (layer, 0, 0, 0)),
                  pl.BlockSpec((None,) + b_s_col.shape[1:], lambda i: (layer, 0, 0, 0))],
        out_specs=pl.BlockSpec((tm, tn), lambda i: (i, 0)),
        out_shape=jax.ShapeDtypeStruct((n, tn), BF16),
        compiler_params=_params(("parallel",), vmem),
        name="uv_sgu",
    )(xb, w_in_b, w_in_b, ln_g, ln_b, w_s_b, b_s_col)


def _na_edge_tiles():
    return -(-(NA_KH // 2) // NA_Q_ROWS)


def _na_bias_table(rpb_l):
    n_edge = _na_edge_tiles()
    rows = 2 * NA_KH + NA_Q_ROWS * (2 * n_edge + 1)
    n_tiles = rows // NA_Q_ROWS
    tiles = list(range(n_edge)) + [n_edge] + list(range(n_tiles - n_edge, n_tiles))
    c = np.arange(GRID_W)
    c0 = np.clip(c - NA_KW // 2, 0, GRID_W - NA_KW)
    col_ok = (c[None, :] >= c0[:, None]) & (c[None, :] < c0[:, None] + NA_KW)
    dc = c[None, :] - c[:, None] + (NA_KW - 1)
    onehot = ((dc[None] == np.arange(2 * NA_KW - 1)[:, None, None]) & col_ok[None]).astype(np.float32)
    toep = jnp.einsum("hrd,dck->hcrk", rpb_l, onehot, precision=lax.Precision.HIGHEST)
    toep = jnp.where(col_ok[None, :, None, :], toep, NEG_BIG)
    pad = NA_BAND
    toep = jnp.pad(toep, ((0, 0), (0, 0), (pad, pad), (0, 0)), constant_values=NEG_BIG)
    toep = toep.reshape(NA_HEADS, GRID_W, -1)
    cases = []
    for rt in tiles:
        rs = rt * NA_Q_ROWS
        b0 = int(np.clip(rs - NA_KH // 2, 0, rows - NA_BAND))
        krow = b0 + np.arange(NA_BAND)
        per_row = []
        for i in range(NA_Q_ROWS):
            qrow = rs + i
            r0 = int(np.clip(qrow - NA_KH // 2, 0, rows - NA_KH))
            ok = np.repeat((krow >= r0) & (krow < r0 + NA_KH), GRID_W)
            start = (b0 - qrow + (NA_KH - 1) + pad) * GRID_W
            blk = toep[:, :, start:start + NA_BAND * GRID_W]
            per_row.append(jnp.where(ok[None, None, :], blk, NEG_BIG))
        cases.append(jnp.stack(per_row, axis=1))
    tab = jnp.stack(cases, axis=0)
    return tab.reshape(len(tiles), NA_HEADS // 2, 2 * NA_Q_ROWS * GRID_W, NA_BAND * GRID_W)


def _na_kernel(q_ref, k_ref, v_ref, bias_ref, o_ref, *, rows):
    tq, tk = NA_Q_ROWS * GRID_W, NA_BAND * GRID_W
    n_sub = NA_STEP_ROWS // NA_Q_ROWS
    n_edge = _na_edge_tiles()
    n_tiles = rows // NA_Q_ROWS
    t0 = pl.program_id(2) * n_sub
    lane = lax.broadcasted_iota(jnp.int32, (1, LANES), 1)
    first_head = lane < NA_HEAD_DIM

    for t in range(n_sub):
        rt = t0 + t
        b0 = jnp.clip(rt * NA_Q_ROWS - NA_KH // 2, 0, rows - NA_BAND)
        koff = pl.multiple_of(b0 * GRID_W, GRID_W)
        kb = k_ref[pl.ds(koff, tk), :]
        vb = v_ref[pl.ds(koff, tk), :]
        q = q_ref[t * tq:(t + 1) * tq, :]
        case = jnp.where(rt < n_edge, rt,
                         jnp.where(rt >= n_tiles - n_edge, rt - (n_tiles - n_edge) + n_edge + 1, n_edge))
        zero = jnp.zeros_like(q)
        q2 = jnp.concatenate([jnp.where(first_head, q, zero), jnp.where(first_head, zero, q)], axis=0)
        q2 = q2 * (NA_HEAD_DIM ** -0.5)
        s = lax.dot_general(q2, kb, (((1,), (1,)), ((), ())), preferred_element_type=F32)
        s = s + bias_ref[case]
        m = jnp.max(s, axis=-1, keepdims=True)
        e = jnp.exp(s - m)
        denom = jnp.sum(e, axis=-1, keepdims=True)
        o2 = jnp.dot(e.astype(BF16), vb, preferred_element_type=F32) * (1.0 / denom)
        o_ref[t * tq:(t + 1) * tq, :] = jnp.where(first_head, o2[:tq], o2[tq:]).astype(BF16)


def _natten(qkv, bias_tab, batch, seq):
    n = qkv.shape[0]
    rows = seq // GRID_W
    assert rows % NA_STEP_ROWS == 0 and rows >= 2 * NA_KH + NA_Q_ROWS * (2 * _na_edge_tiles() + 1)
    d_na = NA_HEADS * NA_HEAD_DIM
    n_pairs = d_na // LANES
    steps = rows // NA_STEP_ROWS
    tqs = NA_STEP_ROWS * GRID_W
    tq, tk = NA_Q_ROWS * GRID_W, NA_BAND * GRID_W
    n_case = bias_tab.shape[0]
    vmem = 2 * (2 * tqs * LANES * 2 + 2 * seq * LANES * 2 + n_case * 2 * tq * tk * 4) + 24 * tq * tk * 4
    return pl.pallas_call(
        functools.partial(_na_kernel, rows=rows),
        grid=(batch, n_pairs, steps),
        in_specs=[
            pl.BlockSpec((tqs, LANES), lambda b, hp, r: (b * steps + r, hp)),
            pl.BlockSpec((seq, LANES), lambda b, hp, r: (b, n_pairs + hp)),
            pl.BlockSpec((seq, LANES), lambda b, hp, r: (b, 2 * n_pairs + hp)),
            pl.BlockSpec((n_case, None, 2 * tq, tk), lambda b, hp, r: (0, hp, 0, 0)),
        ],
        out_specs=pl.BlockSpec((tqs, LANES), lambda b, hp, r: (b * steps + r, hp)),
        out_shape=jax.ShapeDtypeStruct((n, d_na), BF16),
        compiler_params=_params(("parallel", "parallel", "arbitrary"), vmem),
        name="natten",
    )(qkv, qkv, qkv, bias_tab)


def _merge_kernel(x_ref, pa_ref, pb_ref, pc_ref, wg0_ref, wg1_ref, wg2_ref, wa_ref, wb_ref, wc_ref, o_ref):
    x = x_ref[...]
    acc = None
    for br_ref, wg_ref, wbr_ref in ((pa_ref, wg0_ref, wa_ref), (pb_ref, wg1_ref, wb_ref), (pc_ref, wg2_ref, wc_ref)):
        gate = jax.nn.sigmoid(jnp.dot(x, wg_ref[...], preferred_element_type=F32))
        y = jnp.dot(br_ref[...], wbr_ref[...], preferred_element_type=F32)
        acc = gate * y if acc is None else acc + gate * y
    o_ref[...] = acc.astype(BF16)


def _merge(xb, pa, pb, pc, w_in_b, w_a, w_b, w_c, layer):
    n, d = xb.shape
    dbr = pa.shape[1]
    tm, tn = MERGE_TM, MERGE_TN
    gate0 = (w_in_b.shape[2] - N_BRANCH * d) // tn
    per = d // tn
    vmem = 2 * (tm * d * 2 + 3 * tm * dbr * 2 + 3 * d * tn * 2 + 3 * dbr * tn * 2 + tm * tn * 2) + 4 * tm * tn * 4
    gate_spec = lambda b: pl.BlockSpec((None, d, tn), lambda i, j: (layer, 0, gate0 + b * per + j))
    br_spec = pl.BlockSpec((tm, dbr), lambda i, j: (i, 0))
    wbr_spec = pl.BlockSpec((None, dbr, tn), lambda i, j: (layer, 0, j))
    return pl.pallas_call(
        _merge_kernel,
        grid=(n // tm, per),
        in_specs=[pl.BlockSpec((tm, d), lambda i, j: (i, 0)), br_spec, br_spec, br_spec,
                  gate_spec(0), gate_spec(1), gate_spec(2), wbr_spec, wbr_spec, wbr_spec],
        out_specs=pl.BlockSpec((tm, tn), lambda i, j: (i, j)),
        out_shape=jax.ShapeDtypeStruct((n, d), BF16),
        compiler_params=_params(("parallel", "arbitrary"), vmem),
        name="merge",
    )(xb, pa, pb, pc, w_in_b, w_in_b, w_in_b, w_a, w_b, w_c)


def _out_ln_kernel(m_ref, x_ref, w_ref, g_ref, b_ref, of_ref, ob_ref, *, alpha):
    for r in range(m_ref.shape[0] // OUT_ROWS):
        rows = slice(r * OUT_ROWS, (r + 1) * OUT_ROWS)
        y = alpha * x_ref[rows, :] + jnp.dot(m_ref[rows, :], w_ref[...], preferred_element_type=F32)
        y = _layer_norm(y, g_ref[...], b_ref[...])
        of_ref[rows, :] = y
        ob_ref[rows, :] = y.astype(BF16)


def _out_ln(merged, xf, w_out_b, g, b, layer, alpha):
    n, d = xf.shape
    tm = OUT_TM
    vmem = 2 * (tm * d * 2 + tm * d * 4 + d * d * 2 + tm * d * 4 + tm * d * 2) + 4 * OUT_ROWS * d * 4
    row = pl.BlockSpec((tm, d), lambda i: (i, 0))
    vec = pl.BlockSpec((None, 1, d), lambda i: (layer, 0, 0))
    return pl.pallas_call(
        functools.partial(_out_ln_kernel, alpha=alpha),
        grid=(n // tm,),
        in_specs=[row, row, pl.BlockSpec((None, d, d), lambda i: (layer, 0, 0)), vec, vec],
        out_specs=[row, row],
        out_shape=[jax.ShapeDtypeStruct((n, d), F32), jax.ShapeDtypeStruct((n, d), BF16)],
        compiler_params=_params(("parallel",), vmem),
        name="out_ln",
    )(merged, xf, w_out_b, g, b)


def _ffn_kernel(xb_ref, xf_ref, wu_ref, wd_ref, g_ref, b_ref, of_ref, *, alpha):
    f = pl.program_id(1)

    @pl.when(f == 0)
    def _():
        of_ref[...] = alpha * xf_ref[...]

    h = jnp.maximum(jnp.dot(xb_ref[...], wu_ref[...], preferred_element_type=F32), 0.0)
    of_ref[...] += jnp.dot((h * h).astype(BF16), wd_ref[...], preferred_element_type=F32)

    @pl.when(f == pl.num_programs(1) - 1)
    def _():
        of_ref[...] = _layer_norm(of_ref[...], g_ref[...], b_ref[...])


def _ffn(xb, xf, w_up_b, w_down_b, g, b, layer, alpha):
    n, d = xf.shape
    dff = w_up_b.shape[2]
    tm, tf = FFN_TM, FFN_TF
    vmem = 2 * (tm * d * 2 + tm * d * 4 + d * tf * 2 + tf * d * 2 + tm * d * 4) + 3 * tm * tf * 4
    row = pl.BlockSpec((tm, d), lambda i, f: (i, 0))
    vec = pl.BlockSpec((None, 1, d), lambda i, f: (layer, 0, 0))
    return pl.pallas_call(
        functools.partial(_ffn_kernel, alpha=alpha),
        grid=(n // tm, dff // tf),
        in_specs=[row, row, pl.BlockSpec((None, d, tf), lambda i, f: (layer, 0, f)),
                  pl.BlockSpec((None, tf, d), lambda i, f: (layer, f, 0)), vec, vec],
        out_specs=row,
        out_shape=jax.ShapeDtypeStruct((n, d), F32),
        compiler_params=_params(("parallel", "arbitrary"), vmem),
        name="ffn",
    )(xb, xf, w_up_b, w_down_b, g, b)


def kernel(x_prompt, x_sample, w_in, w_pool, s_pool, sgu_ln_g, sgu_ln_b, w_s, b_s, rpb, w_br_pool, w_br_sgu,
           w_br_na, w_out, ln1_g, ln1_b, w_up, w_down, ln2_g, ln2_b):
    depth = w_in.shape[0]
    alpha = float((2 * depth) ** 0.25)
    w_in_b, w_pool_b, w_s_b = w_in.astype(BF16), w_pool.astype(BF16), w_s.astype(BF16)
    w_a, w_b, w_c = w_br_pool.astype(BF16), w_br_sgu.astype(BF16), w_br_na.astype(BF16)
    w_out_b, w_up_b, w_down_b = w_out.astype(BF16), w_up.astype(BF16), w_down.astype(BF16)
    vec = lambda v: v[:, None, :]
    sgu_g, sgu_b, s_pool_v, b_s_col = vec(sgu_ln_g), vec(sgu_ln_b), vec(s_pool), b_s[..., None]
    ln1 = (vec(ln1_g), vec(ln1_b))
    ln2 = (vec(ln2_g), vec(ln2_b))
    bias_tabs = [_na_bias_table(rpb[l]) for l in range(depth)]

    outs = []
    for x in (x_prompt, x_sample):
        batch, seq, d = x.shape
        xf = x.reshape(batch * seq, d)
        for l in range(depth):
            a, xb = _proj_cast(xf, w_in_b, l)
            pb = _uv_sgu(xb, w_in_b, sgu_g, sgu_b, w_s_b, b_s_col, l)
            qkv = _proj(xb, w_in_b, l, 3, 3, "plain", BF16, sgu_g, sgu_b)
            pa = _pool(a, w_pool_b, s_pool_v, l, seq)
            pc = _natten(qkv, bias_tabs[l], batch, seq)
            merged = _merge(xb, pa, pb, pc, w_in_b, w_a, w_b, w_c, l)
            xf, xb = _out_ln(merged, xf, w_out_b, *ln1, l, alpha)
            xf = _ffn(xb, xf, w_up_b, w_down_b, *ln2, l, alpha)
        outs.append(xf.reshape(batch, seq, d))
    return tuple(outs)
```
